```python
import jax, jax.numpy as jnp
from jax import lax
import numpy as np

D_MODEL = 2048
BATCH = 4
SEQ = 8192
DEPTH = 2

N_MEM = 256
EPS = 1e-6

SC_WIDTH = 1024
SC_KERNEL = 3

SSD_HEADS = 32
SSD_HEAD_DIM = 64
SSD_WIDTH = SSD_HEADS * SSD_HEAD_DIM
SSD_STATE = 128
SSD_GROUPS = 4
SSD_CONV = 4
SSD_CHUNK = 128
SSD_CONV_DIM = SSD_WIDTH + 2 * SSD_GROUPS * SSD_STATE
DT_MIN = 0.001
DT_MAX = 0.1

MLA_HEADS = 16
MLA_Q_RANK = 768
MLA_KV_RANK = 512
MLA_NOPE = 128
MLA_ROPE = 64
MLA_V = 128
MLA_WIDTH = MLA_HEADS * MLA_V
ROPE_THETA = 10000.0
Q_BLOCK = 128

MEM_HEADS = 4
MEM_HEAD_DIM = 128
MEM_WIDTH = MEM_HEADS * MEM_HEAD_DIM

EVEN_SPLITS = (SC_WIDTH, SC_WIDTH, SC_WIDTH, SC_WIDTH,
               SSD_WIDTH, SSD_CONV_DIM, SSD_HEADS, MEM_WIDTH, MEM_WIDTH)
EVEN_IN = sum(EVEN_SPLITS)
EVEN_OUT = SC_WIDTH + SSD_WIDTH + MEM_WIDTH
ODD_SPLITS = (MLA_Q_RANK, MLA_KV_RANK, MLA_ROPE, MLA_WIDTH, MEM_WIDTH, MEM_WIDTH)
ODD_IN = sum(ODD_SPLITS)
ODD_OUT = MLA_WIDTH + MEM_WIDTH

kernel_name = "hybrid_shortconv_ssd_mla_memory"


def _split(t, sizes):
    idx = [int(i) for i in np.cumsum(sizes)[:-1]]
    return jnp.split(t, idx, axis=-1)


def rms_norm(x, w):
    xf = x.astype(jnp.float32)
    y = xf * lax.rsqrt(jnp.mean(xf * xf, axis=-1, keepdims=True) + EPS)
    return (y * w.astype(jnp.float32)).astype(x.dtype)


def causal_depthwise_conv(u, w):
    k_w, c = w.shape
    return lax.conv_general_dilated(
        u, w.reshape(k_w, 1, c).astype(u.dtype), window_strides=(1,),
        padding=[(k_w - 1, 0)], dimension_numbers=("NWC", "WIO", "NWC"),
        feature_group_count=c)


def rope(t, pos):
    half = MLA_ROPE // 2
    inv = ROPE_THETA ** (-jnp.arange(half, dtype=jnp.float32) / half)
    ang = pos.astype(jnp.float32)[..., None] * inv
    cos = jnp.cos(ang)[:, :, None, :]
    sin = jnp.sin(ang)[:, :, None, :]
    t1 = t[..., :half].astype(jnp.float32)
    t2 = t[..., half:].astype(jnp.float32)
    return jnp.concatenate([t1 * cos - t2 * sin, t1 * sin + t2 * cos], axis=-1).astype(t.dtype)


def memory_attention(q, mem_n, w_mk, w_mv):
    b_, l_, _ = q.shape
    qh = q.reshape(b_, l_, MEM_HEADS, MEM_HEAD_DIM)
    k = (mem_n @ w_mk).reshape(b_, -1, MEM_HEADS, MEM_HEAD_DIM)
    v = (mem_n @ w_mv).reshape(b_, -1, MEM_HEADS, MEM_HEAD_DIM)
    s = jnp.einsum('bqhd,bkhd->bhqk', qh, k).astype(jnp.float32) * (MEM_HEAD_DIM ** -0.5)
    p = jax.nn.softmax(s, axis=-1).astype(v.dtype)
    o = jnp.einsum('bhqk,bkhd->bqhd', p, v)
    return o.reshape(b_, l_, MEM_WIDTH)


def ssd_chunked(xs, dt, a, bm, cm):
    b_, l_, h_, p_ = xs.shape
    nc = l_ // SSD_CHUNK
    hpg = h_ // SSD_GROUPS
    xdt = (xs * dt[..., None]).reshape(b_, nc, SSD_CHUNK, SSD_GROUPS, hpg, p_)
    bm = bm.reshape(b_, nc, SSD_CHUNK, SSD_GROUPS, SSD_STATE)
    cm = cm.reshape(b_, nc, SSD_CHUNK, SSD_GROUPS, SSD_STATE)
    a_cum = jnp.cumsum((dt * a).reshape(b_, nc, SSD_CHUNK, SSD_GROUPS, hpg), axis=2)
    seg = a_cum[:, :, :, None] - a_cum[:, :, None, :]
    causal = (jnp.arange(SSD_CHUNK)[:, None] >= jnp.arange(SSD_CHUNK)[None, :])[:, :, None, None]
    decay = jnp.exp(jnp.where(causal, seg, -jnp.inf))
    cb = jnp.einsum('bcqgn,bckgn->bcqkg', cm, bm)
    y_diag = jnp.einsum('bcqkg,bcqkgh,bckghp->bcqghp', cb, decay, xdt)
    decay_to_end = jnp.exp(a_cum[:, :, -1:] - a_cum)
    states = jnp.einsum('bckgn,bckgh,bckghp->bcghpn', bm, decay_to_end, xdt)
    chunk_decay = jnp.exp(a_cum[:, :, -1])

    def step(s, inp):
        st, dec = inp
        return s * dec[..., None, None] + st, s

    init = jnp.zeros((b_, SSD_GROUPS, hpg, p_, SSD_STATE), jnp.float32)
    _, s_enter = lax.scan(step, init, (jnp.moveaxis(states, 1, 0), jnp.moveaxis(chunk_decay, 1, 0)))
    s_enter = jnp.moveaxis(s_enter, 0, 1)
    y_off = jnp.einsum('bcqgn,bcghpn,bcqgh->bcqghp', cm, s_enter, jnp.exp(a_cum))
    return (y_diag + y_off).reshape(b_, l_, h_, p_)


def causal_block_attention(q, k, v):
    b_, l_, h_, dk = q.shape
    nb = l_ // Q_BLOCK
    scale = dk ** -0.5
    qb = jnp.moveaxis(q.reshape(b_, nb, Q_BLOCK, h_, dk), 1, 0)
    k_pos = jnp.arange(l_)

    def one_block(args):
        q_blk, i = args
        s = jnp.einsum('bqhd,bkhd->bhqk', q_blk, k).astype(jnp.float32) * scale
        q_pos = i * Q_BLOCK + jnp.arange(Q_BLOCK)
        s = jnp.where(k_pos[None, :] <= q_pos[:, None], s, -jnp.inf)
        p = jax.nn.softmax(s, axis=-1).astype(v.dtype)
        return jnp.einsum('bhqk,bkhv->bqhv', p, v)

    o = lax.map(one_block, (qb, jnp.arange(nb)))
    return jnp.moveaxis(o, 0, 1).reshape(b_, l_, h_ * v.shape[-1])


def even_layer(x, mem_n, norm_w, w_in, sc_conv_w, ssd_conv_w, ssd_conv_b, ssd_dt_bias,
               ssd_a_log, ssd_d, ssd_norm_w, w_mk, w_mv, w_out):
    b_, l_, _ = x.shape
    h = rms_norm(x, norm_w)
    proj = h @ w_in
    sc_b, sc_c, sc_v, sc_z, ssd_z, ssd_xbc, ssd_dt, mq, mz = _split(proj, EVEN_SPLITS)
    y_a = sc_b * causal_depthwise_conv(sc_c * sc_v, sc_conv_w)
    y_a = y_a * jax.nn.silu(sc_z)
    xbc = jax.nn.silu(causal_depthwise_conv(ssd_xbc, ssd_conv_w) + ssd_conv_b)
    xs, bm, cm = _split(xbc, (SSD_WIDTH, SSD_GROUPS * SSD_STATE, SSD_GROUPS * SSD_STATE))
    xs = xs.reshape(b_, l_, SSD_HEADS, SSD_HEAD_DIM).astype(jnp.float32)
    bm = bm.reshape(b_, l_, SSD_GROUPS, SSD_STATE).astype(jnp.float32)
    cm = cm.reshape(b_, l_, SSD_GROUPS, SSD_STATE).astype(jnp.float32)
    dt = jax.nn.softplus(ssd_dt.astype(jnp.float32) + ssd_dt_bias.astype(jnp.float32))
    a = -jnp.exp(ssd_a_log.astype(jnp.float32))
    y_b = ssd_chunked(xs, dt, a, bm, cm) + xs * ssd_d.astype(jnp.float32)[:, None]
    y_b = y_b.reshape(b_, l_, SSD_WIDTH) * jax.nn.silu(ssd_z.astype(jnp.float32))
    y_b = y_b.reshape(b_, l_, SSD_GROUPS, SSD_WIDTH // SSD_GROUPS)
    y_b = y_b * lax.rsqrt(jnp.mean(y_b * y_b, axis=-1, keepdims=True) + EPS)
    y_b = (y_b.reshape(b_, l_, SSD_WIDTH) * ssd_norm_w.astype(jnp.float32)).astype(x.dtype)
    y_m = memory_attention(mq, mem_n, w_mk, w_mv) * jax.nn.silu(mz)
    return x + jnp.concatenate([y_a, y_b, y_m], axis=-1) @ w_out


def odd_layer(x, mem_n, positions, norm_w, w_in, mla_q_norm_w, mla_kv_norm_w, mla_w_uq,
              mla_w_ukv, w_mk, w_mv, w_out):
    b_, l_, _ = x.shape
    h = rms_norm(x, norm_w)
    proj = h @ w_in
    c_q, c_kv, k_r, z_c, mq, mz = _split(proj, ODD_SPLITS)
    q = (rms_norm(c_q, mla_q_norm_w) @ mla_w_uq).reshape(b_, l_, MLA_HEADS, MLA_NOPE + MLA_ROPE)
    kv = (rms_norm(c_kv, mla_kv_norm_w) @ mla_w_ukv).reshape(b_, l_, MLA_HEADS, MLA_NOPE + MLA_V)
    q_nope, q_rope = q[..., :MLA_NOPE], rope(q[..., MLA_NOPE:], positions)
    k_nope, v = kv[..., :MLA_NOPE], kv[..., MLA_NOPE:]
    k_rope = jnp.broadcast_to(rope(k_r[:, :, None, :], positions), (b_, l_, MLA_HEADS, MLA_ROPE))
    q_full = jnp.concatenate([q_nope, q_rope], axis=-1)
    k_full = jnp.concatenate([k_nope, k_rope], axis=-1)
    y_c = causal_block_attention(q_full, k_full, v) * jax.nn.silu(z_c)
    y_m = memory_attention(mq, mem_n, w_mk, w_mv) * jax.nn.silu(mz)
    return x + jnp.concatenate([y_c, y_m], axis=-1) @ w_out


def setup_inputs(seed: int = 0) -> dict:
    key = jax.random.key(seed)
    ks = jax.random.split(key, 32)
    f32 = jnp.float32

    def dense(k, fan_in, fan_out):
        return jax.random.normal(k, (fan_in, fan_out), f32) * fan_in ** -0.5

    def gain(k, n):
        return 1.0 + 0.05 * jax.random.normal(k, (n,), f32)

    x = jax.random.normal(ks[0], (BATCH, SEQ, D_MODEL), f32)
    mem = jax.random.normal(ks[1], (BATCH, N_MEM, D_MODEL), f32)
    offsets = jax.random.randint(ks[2], (BATCH, 1), 0, 4096, dtype=jnp.int32)
    positions = offsets + jnp.arange(SEQ, dtype=jnp.int32)[None, :]
    dt0 = jnp.exp(jax.random.uniform(ks[3], (SSD_HEADS,), f32) * (np.log(DT_MAX) - np.log(DT_MIN)) + np.log(DT_MIN))
    return {
        "x": x,
        "mem": mem,
        "positions": positions,
        "mem_norm_w": gain(ks[4], D_MODEL),
        "norm0_w": gain(ks[5], D_MODEL),
        "w_in0": dense(ks[6], D_MODEL, EVEN_IN),
        "sc_conv_w": jax.random.normal(ks[7], (SC_KERNEL, SC_WIDTH), f32) * SC_KERNEL ** -0.5,
        "ssd_conv_w": jax.random.normal(ks[8], (SSD_CONV, SSD_CONV_DIM), f32) * SSD_CONV ** -0.5,
        "ssd_conv_b": 0.01 * jax.random.normal(ks[9], (SSD_CONV_DIM,), f32),
        "ssd_dt_bias": dt0 + jnp.log(-jnp.expm1(-dt0)),
        "ssd_a_log": jnp.log(jax.random.uniform(ks[10], (SSD_HEADS,), f32, 1.0, 16.0)),
        "ssd_d": 1.0 + 0.1 * jax.random.normal(ks[11], (SSD_HEADS,), f32),
        "ssd_norm_w": gain(ks[12], SSD_WIDTH),
        "mem_k0": dense(ks[13], D_MODEL, MEM_WIDTH),
        "mem_v0": dense(ks[14], D_MODEL, MEM_WIDTH),
        "w_out0": dense(ks[15], EVEN_OUT, D_MODEL),
        "norm1_w": gain(ks[16], D_MODEL),
        "w_in1": dense(ks[17], D_MODEL, ODD_IN),
        "mla_q_norm_w": gain(ks[18], MLA_Q_RANK),
        "mla_kv_norm_w": gain(ks[19], MLA_KV_RANK),
        "mla_w_uq": dense(ks[20], MLA_Q_RANK, MLA_HEADS * (MLA_NOPE + MLA_ROPE)),
        "mla_w_ukv": dense(ks[21], MLA_KV_RANK, MLA_HEADS * (MLA_NOPE + MLA_V)),
        "mem_k1": dense(ks[22], D_MODEL, MEM_WIDTH),
        "mem_v1": dense(ks[23], D_MODEL, MEM_WIDTH),
        "w_out1": dense(ks[24], ODD_OUT, D_MODEL),
        "final_norm_w": gain(ks[25], D_MODEL),
    }


def reference(x, mem, positions, mem_norm_w, norm0_w, w_in0, sc_conv_w, ssd_conv_w, ssd_conv_b,
              ssd_dt_bias, ssd_a_log, ssd_d, ssd_norm_w, mem_k0, mem_v0, w_out0, norm1_w, w_in1,
              mla_q_norm_w, mla_kv_norm_w, mla_w_uq, mla_w_ukv, mem_k1, mem_v1, w_out1, final_norm_w):
    mem_n = rms_norm(mem, mem_norm_w)
    even_params = (norm0_w, w_in0, sc_conv_w, ssd_conv_w, ssd_conv_b, ssd_dt_bias, ssd_a_log,
                   ssd_d, ssd_norm_w, mem_k0, mem_v0, w_out0)
    odd_params = (norm1_w, w_in1, mla_q_norm_w, mla_kv_norm_w, mla_w_uq, mla_w_ukv,
                  mem_k1, mem_v1, w_out1)
    layer_params = [even_params, odd_params]
    for layer in range(DEPTH):
        if layer % 2 == 0:
            x = even_layer(x, mem_n, *layer_params[layer])
        else:
            x = odd_layer(x, mem_n, positions, *layer_params[layer])
    return rms_norm(x, final_norm_w)
```

```python
import functools

import numpy as np
import jax
import jax.numpy as jnp
from jax import lax
from jax.experimental import pallas as pl
from jax.experimental.pallas import tpu as pltpu

F32 = jnp.float32
BF16 = jnp.bfloat16

D_MODEL = 2048
N_MEM = 256
EPS = 1e-6
SC_WIDTH = 1024
SC_KERNEL = 3
SSD_HEADS = 32
SSD_HEAD_DIM = 64
SSD_WIDTH = SSD_HEADS * SSD_HEAD_DIM
SSD_STATE = 128
SSD_GROUPS = 4
SSD_CONV = 4
SSD_CHUNK = 128
SSD_CONV_DIM = SSD_WIDTH + 2 * SSD_GROUPS * SSD_STATE
SSD_GROUP_WIDTH = SSD_WIDTH // SSD_GROUPS
MLA_HEADS = 16
MLA_Q_RANK = 768
MLA_KV_RANK = 512
MLA_NOPE = 128
MLA_ROPE = 64
MLA_V = 128
MLA_WIDTH = MLA_HEADS * MLA_V
MLA_QK_PAD = 256
ROPE_THETA = 10000.0
MEM_HEADS = 4
MEM_HEAD_DIM = 128
MEM_WIDTH = MEM_HEADS * MEM_HEAD_DIM
EVEN_MAIN = 4 * SC_WIDTH + SSD_WIDTH + SSD_CONV_DIM + 2 * MEM_WIDTH
EVEN_OUT = SC_WIDTH + SSD_WIDTH + MEM_WIDTH
ODD_MAIN = MLA_Q_RANK + MLA_KV_RANK + MLA_WIDTH + 2 * MEM_WIDTH
ODD_OUT = MLA_WIDTH + MEM_WIDTH
LANES = 128
HALO = 8

VMEM_LIMIT = 56 * 1024 * 1024


def _cparams(sem):
    return pltpu.CompilerParams(dimension_semantics=sem, vmem_limit_bytes=VMEM_LIMIT)


def _dot(a, b):
    return jnp.dot(a, b, preferred_element_type=F32)


def _dot_nt(a, b):
    return lax.dot_general(a, b, (((1,), (1,)), ((), ())), preferred_element_type=F32)


def _rms(x, w):
    return x * lax.rsqrt(jnp.mean(x * x, axis=-1, keepdims=True) + EPS) * w


def _silu(z):
    return z * (1.0 / (1.0 + jnp.exp(-z)))


def _split_bf16(v, n):
    parts = []
    r = v
    for _ in range(n):
        p = r.astype(BF16)
        parts.append(p)
        r = r - p.astype(F32)
    return parts


def _expand(v, e_ref, n):
    e = e_ref[...]
    out = None
    for p in _split_bf16(v, n):
        t = _dot(p, e)
        out = t if out is None else out + t
    return out


def _mem_kv_kernel(mem_ref, nw_ref, w_ref, out_ref):
    m = _rms(mem_ref[0], nw_ref[...])
    out_ref[0] = _dot(m.astype(BF16), w_ref[...]).astype(BF16)


def _mem_kv(mem, mem_norm_w, w_kv):
    b, m, d = mem.shape
    n = w_kv.shape[1]
    tn = MEM_WIDTH
    return pl.pallas_call(
        _mem_kv_kernel,
        grid=(b, n // tn),
        in_specs=[pl.BlockSpec((1, m, d), lambda i, j: (i, 0, 0)),
                  pl.BlockSpec((1, d), lambda i, j: (0, 0)),
                  pl.BlockSpec((d, tn), lambda i, j: (0, j))],
        out_specs=pl.BlockSpec((1, m, tn), lambda i, j: (i, 0, j)),
        out_shape=jax.ShapeDtypeStruct((b, m, n), BF16),
        compiler_params=_cparams(("parallel", "arbitrary")),
        name="mem_kv",
    )(mem, mem_norm_w.reshape(1, d), w_kv)


def _norm_matmul_kernel(x_ref, nw_ref, w_ref, ws_ref, out_ref, side_ref, xn_ref):
    @pl.when(pl.program_id(1) == 0)
    def _():
        xn_ref[...] = _rms(x_ref[...], nw_ref[...]).astype(BF16)
        side_ref[...] = _dot(xn_ref[...], ws_ref[...])

    out_ref[...] = _dot(xn_ref[...], w_ref[...]).astype(out_ref.dtype)


def _norm_matmul(x, norm_w, w, w_side, tm, tn):
    n, d = x.shape
    c = w.shape[1]
    cs = w_side.shape[1]
    return pl.pallas_call(
        _norm_matmul_kernel,
        grid=(n // tm, c // tn),
        in_specs=[pl.BlockSpec((tm, d), lambda i, j: (i, 0)),
                  pl.BlockSpec((1, d), lambda i, j: (0, 0)),
                  pl.BlockSpec((d, tn), lambda i, j: (0, j)),
                  pl.BlockSpec((d, cs), lambda i, j: (0, 0))],
        out_specs=[pl.BlockSpec((tm, tn), lambda i, j: (i, j)),
                   pl.BlockSpec((tm, cs), lambda i, j: (i, 0))],
        out_shape=[jax.ShapeDtypeStruct((n, c), BF16),
                   jax.ShapeDtypeStruct((n, cs), F32)],
        scratch_shapes=[pltpu.VMEM((tm, d), BF16)],
        compiler_params=_cparams(("parallel", "arbitrary")),
        name="norm_matmul",
    )(x, norm_w.reshape(1, d), w, w_side)


def _mem_attention(mq, mz, k, v):
    outs = []
    for h in range(MEM_HEADS):
        sl = slice(h * MEM_HEAD_DIM, (h + 1) * MEM_HEAD_DIM)
        q = (mq[:, sl] * (MEM_HEAD_DIM ** -0.5)).astype(BF16)
        s = _dot_nt(q, k[:, sl])
        p = jnp.exp(s - jnp.max(s, axis=-1, keepdims=True))
        o = _dot(p.astype(BF16), v[:, sl]) * (1.0 / jnp.sum(p, axis=-1, keepdims=True))
        outs.append(o * _silu(mz[:, sl]))
    return outs


_O_SCB, _O_SCC, _O_SCV, _O_SCZ = 0, SC_WIDTH, 2 * SC_WIDTH, 3 * SC_WIDTH
_O_SSDZ = 4 * SC_WIDTH
_O_XBC = _O_SSDZ + SSD_WIDTH
_O_MQ0 = _O_XBC + SSD_CONV_DIM
_O_MZ0 = _O_MQ0 + MEM_WIDTH


def _even_mixer_kernel(p_ref, dt_ref, k_ref, v_ref, scw_ref, xw_ref, xb_ref, dtb_ref, alog_ref, d_ref,
                       nw_ref, ehp_ref, ehk_ref, tri_ref, y_ref, cv_buf, xbc_buf, state):
    tile = p_ref.shape[0]

    @pl.when(pl.program_id(1) == 0)
    def _():
        cv_buf[0:HALO, :] = jnp.zeros((HALO, SC_WIDTH), F32)
        xbc_buf[0:HALO, :] = jnp.zeros((HALO, SSD_CONV_DIM), F32)
        state[...] = jnp.zeros(state.shape, F32)

    cv_buf[HALO:HALO + tile, :] = (p_ref[:, _O_SCC:_O_SCC + SC_WIDTH].astype(F32)
                                   * p_ref[:, _O_SCV:_O_SCV + SC_WIDTH].astype(F32))
    conv = None
    for k in range(SC_KERNEL):
        off = HALO - (SC_KERNEL - 1) + k
        t = scw_ref[k:k + 1, :] * cv_buf[off:off + tile, :]
        conv = t if conv is None else conv + t
    y_a = (p_ref[:, _O_SCB:_O_SCB + SC_WIDTH].astype(F32) * conv
           * _silu(p_ref[:, _O_SCZ:_O_SCZ + SC_WIDTH].astype(F32)))
    y_ref[:, 0:SC_WIDTH] = y_a.astype(BF16)
    cv_buf[0:HALO, :] = cv_buf[tile:tile + HALO, :]

    y_m = _mem_attention(p_ref[:, _O_MQ0:_O_MQ0 + MEM_WIDTH].astype(F32),
                         p_ref[:, _O_MZ0:_O_MZ0 + MEM_WIDTH].astype(F32), k_ref[0], v_ref[0])
    for h in range(MEM_HEADS):
        c0 = SC_WIDTH + SSD_WIDTH + h * MEM_HEAD_DIM
        y_ref[:, c0:c0 + MEM_HEAD_DIM] = y_m[h].astype(BF16)

    xbc_buf[HALO:HALO + tile, :] = p_ref[:, _O_XBC:_O_XBC + SSD_CONV_DIM].astype(F32)
    a_row = -jnp.exp(alog_ref[...])
    row = lax.broadcasted_iota(jnp.int32, (SSD_CHUNK, SSD_CHUNK), 0)
    col = lax.broadcasted_iota(jnp.int32, (SSD_CHUNK, SSD_CHUNK), 1)
    causal = row >= col
    first_head = col < SSD_HEAD_DIM
    heads_per_group = SSD_HEADS // SSD_GROUPS
    for c in range(tile // SSD_CHUNK):
        r0 = c * SSD_CHUNK
        u = xb_ref[...]
        for k in range(SSD_CONV):
            off = HALO - (SSD_CONV - 1) + k + r0
            u = u + xw_ref[k:k + 1, :] * xbc_buf[off:off + SSD_CHUNK, :]
        xbc = _silu(u)
        xs = xbc[:, 0:SSD_WIDTH]
        bm = xbc[:, SSD_WIDTH:SSD_WIDTH + SSD_GROUPS * SSD_STATE]
        cm = xbc[:, SSD_WIDTH + SSD_GROUPS * SSD_STATE:]
        dt_in = dt_ref[r0:r0 + SSD_CHUNK, :] + dtb_ref[...]
        dt = jnp.maximum(dt_in, 0.0) + jnp.log1p(jnp.exp(-jnp.abs(dt_in)))
        dta = dt * a_row
        tri = tri_ref[...]
        a_cum = None
        for piece in _split_bf16(dta, 3):
            t = _dot(tri, piece)
            a_cum = t if a_cum is None else a_cum + t
        a_cum_t = a_cum.T
        a_last = a_cum[SSD_CHUNK - 1:SSD_CHUNK, :]
        exp_a = jnp.exp(a_cum)
        dt_e = _expand(dt, ehp_ref, 2)
        exp_a_e = _expand(exp_a, ehp_ref, 2)
        dtd_e = _expand(dt * jnp.exp(a_last - a_cum), ehp_ref, 2)
        a_q = _expand(a_cum, ehk_ref, 3)
        xdt = xs * dt_e
        xdte = xs * dtd_e
        chunk_decay_e = exp_a_e[SSD_CHUNK - 1:SSD_CHUNK, :]
        z = p_ref[r0:r0 + SSD_CHUNK, _O_SSDZ:_O_SSDZ + SSD_WIDTH].astype(F32)
        for g in range(SSD_GROUPS):
            gs = slice(g * SSD_GROUP_WIDTH, (g + 1) * SSD_GROUP_WIDTH)
            bm_g = bm[:, g * SSD_STATE:(g + 1) * SSD_STATE]
            cm_g = cm[:, g * SSD_STATE:(g + 1) * SSD_STATE].astype(BF16)
            cb = _dot_nt(cm_g, bm_g.astype(BF16))
            s_enter = state[g]
            y_off = _dot(cm_g, s_enter.astype(BF16)) * exp_a_e[:, gs]
            state[g] = s_enter * chunk_decay_e[:, gs] + _dot(bm_g.T.astype(BF16), xdte[:, gs].astype(BF16))
            pairs = []
            for j in range(heads_per_group // 2):
                h0 = g * heads_per_group + 2 * j
                lmat = []
                for h in (h0, h0 + 1):
                    seg = a_q[:, h * SSD_CHUNK:(h + 1) * SSD_CHUNK] - a_cum_t[h:h + 1, :]
                    lmat.append((cb * jnp.exp(jnp.where(causal, seg, -jnp.inf))).astype(BF16))
                xp = xdt[:, h0 * SSD_HEAD_DIM:(h0 + 2) * SSD_HEAD_DIM]
                rhs = jnp.concatenate([jnp.where(first_head, xp, 0.0), jnp.where(first_head, 0.0, xp)],
                                      axis=0).astype(BF16)
                pairs.append(_dot(jnp.concatenate(lmat, axis=1), rhs))
            y_g = (jnp.concatenate(pairs, axis=1) + y_off + xs[:, gs] * d_ref[:, gs]) * _silu(z[:, gs])
            y_g = y_g * lax.rsqrt(jnp.mean(y_g * y_g, axis=-1, keepdims=True) + EPS) * nw_ref[:, gs]
            c0 = SC_WIDTH + g * SSD_GROUP_WIDTH
            y_ref[r0:r0 + SSD_CHUNK, c0:c0 + SSD_GROUP_WIDTH] = y_g.astype(BF16)
    xbc_buf[0:HALO, :] = xbc_buf[tile:tile + HALO, :]


def _even_mixer(proj, dt_raw, mem_kv, sc_conv_w, ssd_conv_w, ssd_conv_b, dt_bias, a_log, d_e, ssd_norm_w,
                e_hp, e_hk, tri, batch, tile):
    n = proj.shape[0]
    tiles = n // batch // tile
    const = lambda shape: pl.BlockSpec(shape, lambda b, t: (0,) * len(shape))
    return pl.pallas_call(
        _even_mixer_kernel,
        grid=(batch, tiles),
        in_specs=[pl.BlockSpec((tile, EVEN_MAIN), lambda b, t: (b * tiles + t, 0)),
                  pl.BlockSpec((tile, LANES), lambda b, t: (b * tiles + t, 0)),
                  pl.BlockSpec((1, N_MEM, MEM_WIDTH), lambda b, t: (b, 0, 0)),
                  pl.BlockSpec((1, N_MEM, MEM_WIDTH), lambda b, t: (b, 0, 1)),
                  const((SC_KERNEL, SC_WIDTH)), const((SSD_CONV, SSD_CONV_DIM)), const((1, SSD_CONV_DIM)),
                  const((1, LANES)), const((1, LANES)), const((1, SSD_WIDTH)), const((1, SSD_WIDTH)),
                  const((LANES, SSD_WIDTH)), const((LANES, SSD_HEADS * SSD_CHUNK)),
                  const((SSD_CHUNK, SSD_CHUNK))],
        out_specs=pl.BlockSpec((tile, EVEN_OUT), lambda b, t: (b * tiles + t, 0)),
        out_shape=jax.ShapeDtypeStruct((n, EVEN_OUT), BF16),
        scratch_shapes=[pltpu.VMEM((tile + HALO, SC_WIDTH), F32),
                        pltpu.VMEM((tile + HALO, SSD_CONV_DIM), F32),
                        pltpu.VMEM((SSD_GROUPS, SSD_STATE, SSD_GROUP_WIDTH), F32)],
        compiler_params=_cparams(("parallel", "arbitrary")),
        name="even_mixer",
    )(proj, dt_raw, mem_kv, mem_kv, sc_conv_w, ssd_conv_w, ssd_conv_b, dt_bias, a_log, d_e, ssd_norm_w,
      e_hp, e_hk, tri)


def _residual_matmul_kernel(y_ref, w_ref, x_ref, out_ref):
    out_ref[...] = x_ref[...] + _dot(y_ref[...], w_ref[...])


def _residual_matmul(y, w, x, tm, tn):
    n, k = y.shape
    d = w.shape[1]
    return pl.pallas_call(
        _residual_matmul_kernel,
        grid=(n // tm, d // tn),
        in_specs=[pl.BlockSpec((tm, k), lambda i, j: (i, 0)),
                  pl.BlockSpec((k, tn), lambda i, j: (0, j)),
                  pl.BlockSpec((tm, tn), lambda i, j: (i, j))],
        out_specs=pl.BlockSpec((tm, tn), lambda i, j: (i, j)),
        out_shape=jax.ShapeDtypeStruct((n, d), F32),
        compiler_params=_cparams(("parallel", "arbitrary")),
        name="residual_matmul",
    )(y, w, x)


_O_CQ, _O_CKV = 0, MLA_Q_RANK
_O_ZC = MLA_Q_RANK + MLA_KV_RANK
_O_MQ1 = _O_ZC + MLA_WIDTH
_O_MZ1 = _O_MQ1 + MEM_WIDTH


def _rope_lanes(a, cos_t, sin_t, low_half):
    half = MLA_ROPE // 2
    swapped = jnp.where(low_half, pltpu.roll(a, LANES - half, 1), pltpu.roll(a, half, 1))
    return a * cos_t + swapped * sin_t


def _mla_qkv_kernel(p_ref, kr_ref, pos_ref, qnw_ref, kvnw_ref, wq_ref, wkv_ref, freq_ref, sign_ref,
                    q_ref, k_ref, v_ref):
    tile = p_ref.shape[0]
    cq = _rms(p_ref[:, _O_CQ:_O_CQ + MLA_Q_RANK].astype(F32), qnw_ref[...])
    q = _dot(cq.astype(BF16), wq_ref[...])
    ckv = _rms(p_ref[:, _O_CKV:_O_CKV + MLA_KV_RANK].astype(F32), kvnw_ref[...])
    kv = _dot(ckv.astype(BF16), wkv_ref[...])
    ang = pos_ref[...].astype(F32) * freq_ref[...]
    cos_t = jnp.cos(ang)
    sin_t = jnp.sin(ang) * sign_ref[...]
    low_half = lax.broadcasted_iota(jnp.int32, (tile, LANES), 1) < MLA_ROPE // 2
    k_rope = _rope_lanes(kr_ref[...], cos_t, sin_t, low_half).astype(BF16)
    scale = (MLA_NOPE + MLA_ROPE) ** -0.5
    for h in range(MLA_HEADS):
        nope = slice(h * MLA_NOPE, (h + 1) * MLA_NOPE)
        rope = slice(MLA_WIDTH + h * LANES, MLA_WIDTH + (h + 1) * LANES)
        q_ref[0, h, :, 0:MLA_NOPE] = (q[:, nope] * scale).astype(BF16)
        q_ref[0, h, :, MLA_NOPE:MLA_QK_PAD] = (_rope_lanes(q[:, rope], cos_t, sin_t, low_half) * scale).astype(BF16)
        k_ref[0, h, :, 0:MLA_NOPE] = kv[:, nope].astype(BF16)
        k_ref[0, h, :, MLA_NOPE:MLA_QK_PAD] = k_rope
        v_ref[0, h] = kv[:, MLA_WIDTH + h * MLA_V:MLA_WIDTH + (h + 1) * MLA_V].astype(BF16)


def _mla_qkv(proj, kr, pos, q_norm_w, kv_norm_w, w_uq, w_ukv, freq, sign, batch, tile):
    n = proj.shape[0]
    seq = n // batch
    tiles = seq // tile
    const = lambda shape: pl.BlockSpec(shape, lambda b, t: (0,) * len(shape))
    head_spec = lambda w: pl.BlockSpec((1, MLA_HEADS, tile, w), lambda b, t: (b, 0, t, 0))
    return pl.pallas_call(
        _mla_qkv_kernel,
        grid=(batch, tiles),
        in_specs=[pl.BlockSpec((tile, MLA_Q_RANK + MLA_KV_RANK), lambda b, t: (b * tiles + t, 0)),
                  pl.BlockSpec((tile, LANES), lambda b, t: (b * tiles + t, 0)),
                  pl.BlockSpec((tile, 1), lambda b, t: (b * tiles + t, 0)),
                  const((1, MLA_Q_RANK)), const((1, MLA_KV_RANK)),
                  const((MLA_Q_RANK, MLA_WIDTH + MLA_HEADS * LANES)), const((MLA_KV_RANK, 2 * MLA_WIDTH)),
                  const((1, LANES)), const((1, LANES))],
        out_specs=[head_spec(MLA_QK_PAD), head_spec(MLA_QK_PAD), head_spec(MLA_V)],
        out_shape=[jax.ShapeDtypeStruct((batch, MLA_HEADS, seq, MLA_QK_PAD), BF16),
                   jax.ShapeDtypeStruct((batch, MLA_HEADS, seq, MLA_QK_PAD), BF16),
                   jax.ShapeDtypeStruct((batch, MLA_HEADS, seq, MLA_V), BF16)],
        compiler_params=_cparams(("parallel", "parallel")),
        name="mla_qkv",
    )(proj, kr, pos, q_norm_w, kv_norm_w, w_uq, w_ukv, freq, sign)


def _causal_attention_kernel(q_ref, k_ref, v_ref, o_ref, *, tk):
    tq = q_ref.shape[2]
    q = q_ref[0, 0]
    qi = pl.program_id(2)

    def step(j, carry, masked_offset=None):
        m, l, acc = carry
        start = pl.multiple_of(j * tk, tk)
        s = _dot_nt(q, k_ref[0, 0, pl.ds(start, tk), :])
        if masked_offset is not None:
            row = lax.broadcasted_iota(jnp.int32, (tq, tk), 0)
            col = lax.broadcasted_iota(jnp.int32, (tq, tk), 1) + masked_offset
            s = jnp.where(col <= row, s, -1e30)
        m_new = jnp.maximum(m, jnp.max(s, axis=-1, keepdims=True))
        alpha = jnp.exp(m - m_new)
        p = jnp.exp(s - m_new)
        l = alpha * l + jnp.sum(p, axis=-1, keepdims=True)
        acc = alpha * acc + _dot(p.astype(BF16), v_ref[0, 0, pl.ds(start, tk), :])
        return m_new, l, acc

    blocks = tq // tk
    carry = (jnp.full((tq, 1), -1e30, F32), jnp.zeros((tq, 1), F32), jnp.zeros((tq, MLA_V), F32))
    carry = lax.fori_loop(0, qi * blocks, step, carry)
    for d in range(blocks):
        carry = step(qi * blocks + d, carry, masked_offset=d * tk)
    _, l, acc = carry
    o_ref[...] = (acc * (1.0 / l)).astype(o_ref.dtype)


def _causal_attention(q, k, v, tq, tk):
    batch, heads, seq, _ = q.shape
    nq = seq // tq
    return pl.pallas_call(
        functools.partial(_causal_attention_kernel, tk=tk),
        grid=(batch, heads, nq),
        in_specs=[pl.BlockSpec((1, 1, tq, MLA_QK_PAD), lambda b, h, i: (b, h, i, 0)),
                  pl.BlockSpec((1, 1, seq, MLA_QK_PAD), lambda b, h, i: (b, h, 0, 0)),
                  pl.BlockSpec((1, 1, seq, MLA_V), lambda b, h, i: (b, h, 0, 0))],
        out_specs=pl.BlockSpec((tq, MLA_V), lambda b, h, i: (b * nq + i, h)),
        out_shape=jax.ShapeDtypeStruct((batch * seq, heads * MLA_V), BF16),
        compiler_params=_cparams(("parallel", "parallel", "arbitrary")),
        name="causal_attention",
    )(q, k, v)


def _odd_tail_kernel(a_ref, p_ref, k_ref, v_ref, x_ref, w_ref, fw_ref, out_ref):
    y_c = a_ref[...].astype(F32) * _silu(p_ref[:, _O_ZC:_O_ZC + MLA_WIDTH].astype(F32))
    acc = x_ref[...] + _dot(y_c.astype(BF16), w_ref[0:MLA_WIDTH, :])
    y_m = _mem_attention(p_ref[:, _O_MQ1:_O_MQ1 + MEM_WIDTH].astype(F32),
                         p_ref[:, _O_MZ1:_O_MZ1 + MEM_WIDTH].astype(F32), k_ref[0], v_ref[0])
    y_m = jnp.concatenate(y_m, axis=1).astype(BF16)
    acc = acc + _dot(y_m, w_ref[MLA_WIDTH:ODD_OUT, :])
    out_ref[...] = _rms(acc, fw_ref[...])


def _odd_tail(attn, proj, mem_kv, x, w_out, final_w, batch, tile):
    n = x.shape[0]
    tiles = n // batch // tile
    const = lambda shape: pl.BlockSpec(shape, lambda b, t: (0,) * len(shape))
    return pl.pallas_call(
        _odd_tail_kernel,
        grid=(batch, tiles),
        in_specs=[pl.BlockSpec((tile, MLA_WIDTH), lambda b, t: (b * tiles + t, 0)),
                  pl.BlockSpec((tile, ODD_MAIN), lambda b, t: (b * tiles + t, 0)),
                  pl.BlockSpec((1, N_MEM, MEM_WIDTH), lambda b, t: (b, 0, 2)),
                  pl.BlockSpec((1, N_MEM, MEM_WIDTH), lambda b, t: (b, 0, 3)),
                  pl.BlockSpec((tile, D_MODEL), lambda b, t: (b * tiles + t, 0)),
                  const((ODD_OUT, D_MODEL)), const((1, D_MODEL))],
        out_specs=pl.BlockSpec((tile, D_MODEL), lambda b, t: (b * tiles + t, 0)),
        out_shape=jax.ShapeDtypeStruct((n, D_MODEL), F32),
        compiler_params=_cparams(("parallel", "parallel")),
        name="odd_tail",
    )(attn, proj, mem_kv, mem_kv, x, w_out, final_w)


def _pad_cols(w, width):
    return jnp.pad(w, ((0, 0), (0, width - w.shape[1])))


def _expansion_matrix(width_per_head):
    e = np.zeros((LANES, SSD_HEADS * width_per_head), np.float32)
    for h in range(SSD_HEADS):
        e[h, h * width_per_head:(h + 1) * width_per_head] = 1.0
    return jnp.asarray(e, BF16)


def kernel(x, mem, positions, mem_norm_w, norm0_w, w_in0, sc_conv_w, ssd_conv_w, ssd_conv_b, ssd_dt_bias,
           ssd_a_log, ssd_d, ssd_norm_w, mem_k0, mem_v0, w_out0, norm1_w, w_in1, mla_q_norm_w,
           mla_kv_norm_w, mla_w_uq, mla_w_ukv, mem_k1, mem_v1, w_out1, final_norm_w):
    batch, seq, d = x.shape
    n = batch * seq
    x2d = x.reshape(n, d)
    row = lambda v: v.reshape(1, -1).astype(F32)

    mem_kv = _mem_kv(mem, mem_norm_w, jnp.concatenate([mem_k0, mem_v0, mem_k1, mem_v1], axis=1).astype(BF16))

    o_dt = _O_MQ0
    w0_main = jnp.concatenate([w_in0[:, :o_dt], w_in0[:, o_dt + SSD_HEADS:]], axis=1).astype(BF16)
    w0_dt = _pad_cols(w_in0[:, o_dt:o_dt + SSD_HEADS], LANES).astype(BF16)
    proj0, dt_raw = _norm_matmul(x2d, norm0_w, w0_main, w0_dt, tm=min(1024, n), tn=1024)
    pad_heads = lambda v: _pad_cols(row(v), LANES)
    y0 = _even_mixer(proj0, dt_raw, mem_kv, sc_conv_w, ssd_conv_w, row(ssd_conv_b), pad_heads(ssd_dt_bias),
                     pad_heads(ssd_a_log), row(jnp.repeat(ssd_d, SSD_HEAD_DIM)), row(ssd_norm_w),
                     _expansion_matrix(SSD_HEAD_DIM), _expansion_matrix(SSD_CHUNK),
                     jnp.asarray(np.tril(np.ones((SSD_CHUNK, SSD_CHUNK), np.float32)), BF16),
                     batch, tile=min(256, seq))
    x1 = _residual_matmul(y0, w_out0.astype(BF16), x2d, tm=min(512, n), tn=1024)

    o_kr = MLA_Q_RANK + MLA_KV_RANK
    w1_main = jnp.concatenate([w_in1[:, :o_kr], w_in1[:, o_kr + MLA_ROPE:]], axis=1).astype(BF16)
    w1_kr = _pad_cols(w_in1[:, o_kr:o_kr + MLA_ROPE], LANES).astype(BF16)
    proj1, kr = _norm_matmul(x1, norm1_w, w1_main, w1_kr, tm=min(512, n), tn=ODD_MAIN // 2)
    wq = mla_w_uq.reshape(MLA_Q_RANK, MLA_HEADS, MLA_NOPE + MLA_ROPE)
    wq_rope = jnp.pad(wq[:, :, MLA_NOPE:], ((0, 0), (0, 0), (0, LANES - MLA_ROPE)))
    wq = jnp.concatenate([wq[:, :, :MLA_NOPE].reshape(MLA_Q_RANK, MLA_WIDTH),
                          wq_rope.reshape(MLA_Q_RANK, MLA_HEADS * LANES)], axis=1).astype(BF16)
    wkv = mla_w_ukv.reshape(MLA_KV_RANK, MLA_HEADS, MLA_NOPE + MLA_V)
    wkv = jnp.concatenate([wkv[:, :, :MLA_NOPE].reshape(MLA_KV_RANK, MLA_WIDTH),
                           wkv[:, :, MLA_NOPE:].reshape(MLA_KV_RANK, MLA_WIDTH)], axis=1).astype(BF16)
    half = MLA_ROPE // 2
    inv = ROPE_THETA ** (-jnp.arange(half, dtype=F32) / half)
    freq = jnp.concatenate([inv, inv, jnp.zeros((LANES - MLA_ROPE,), F32)]).reshape(1, LANES)
    sign = np.zeros((1, LANES), np.float32)
    sign[0, :half] = -1.0
    sign[0, half:MLA_ROPE] = 1.0
    q, k, v = _mla_qkv(proj1, kr, positions.reshape(n, 1), row(mla_q_norm_w), row(mla_kv_norm_w), wq, wkv,
                       freq, jnp.asarray(sign), batch, tile=min(512, seq))
    attn = _causal_attention(q, k, v, tq=min(512, seq), tk=min(512, seq))
    out = _odd_tail(attn, proj1, mem_kv, x1, w_out1.astype(BF16), row(final_norm_w), batch, tile=min(256, seq))
    return out.reshape(batch, seq, d)
```

```python
import functools

import numpy as np
import jax
import jax.numpy as jnp
from jax import lax
from jax.experimental import pallas as pl
from jax.experimental.pallas import tpu as pltpu

F32 = jnp.float32
BF16 = jnp.bfloat16

D_MODEL = 2048
N_MEM = 256
EPS = 1e-6
SC_WIDTH = 1024
SC_KERNEL = 3
SSD_HEADS = 32
SSD_HEAD_DIM = 64
SSD_WIDTH = SSD_HEADS * SSD_HEAD_DIM
SSD_STATE = 128
SSD_GROUPS = 4
SSD_CONV = 4
SSD_CHUNK = 128
SSD_CONV_DIM = SSD_WIDTH + 2 * SSD_GROUPS * SSD_STATE
SSD_GROUP_WIDTH = SSD_WIDTH // SSD_GROUPS
MLA_HEADS = 16
MLA_Q_RANK = 768
MLA_KV_RANK = 512
MLA_NOPE = 128
MLA_ROPE = 64
MLA_V = 128
MLA_WIDTH = MLA_HEADS * MLA_V
MLA_QK_PAD = 256
ROPE_THETA = 10000.0
MEM_HEADS = 4
MEM_HEAD_DIM = 128
MEM_WIDTH = MEM_HEADS * MEM_HEAD_DIM
EVEN_MAIN = 4 * SC_WIDTH + SSD_WIDTH + SSD_CONV_DIM + 2 * MEM_WIDTH
EVEN_OUT = SC_WIDTH + SSD_WIDTH + MEM_WIDTH
ODD_MAIN = MLA_Q_RANK + MLA_KV_RANK + MLA_WIDTH + 2 * MEM_WIDTH
ODD_OUT = MLA_WIDTH + MEM_WIDTH
LANES = 128
HALO = 8

VMEM_LIMIT = 56 * 1024 * 1024


def _cparams(sem):
    return pltpu.CompilerParams(dimension_semantics=sem, vmem_limit_bytes=VMEM_LIMIT)


def _dot(a, b):
    return jnp.dot(a, b, preferred_element_type=F32)


def _dot_nt(a, b):
    return lax.dot_general(a, b, (((1,), (1,)), ((), ())), preferred_element_type=F32)


def _rms(x, w):
    return x * lax.rsqrt(jnp.mean(x * x, axis=-1, keepdims=True) + EPS) * w


def _silu(z):
    return z * (1.0 / (1.0 + jnp.exp(-z)))


def _split_bf16(v, n):
    parts = []
    r = v
    for _ in range(n):
        p = r.astype(BF16)
        parts.append(p)
        r = r - p.astype(F32)
    return parts


def _expand(v, e_ref, n):
    e = e_ref[...]
    out = None
    for p in _split_bf16(v, n):
        t = _dot(p, e)
        out = t if out is None else out + t
    return out


def _mem_kv_kernel(mem_ref, nw_ref, w_ref, out_ref):
    m = _rms(mem_ref[0], nw_ref[...])
    out_ref[0] = _dot(m.astype(BF16), w_ref[...]).astype(BF16)


def _mem_kv(mem, mem_norm_w, w_kv):
    b, m, d = mem.shape
    n = w_kv.shape[1]
    tn = MEM_WIDTH
    return pl.pallas_call(
        _mem_kv_kernel,
        grid=(b, n // tn),
        in_specs=[pl.BlockSpec((1, m, d), lambda i, j: (i, 0, 0)),
                  pl.BlockSpec((1, d), lambda i, j: (0, 0)),
                  pl.BlockSpec((d, tn), lambda i, j: (0, j))],
        out_specs=pl.BlockSpec((1, m, tn), lambda i, j: (i, 0, j)),
        out_shape=jax.ShapeDtypeStruct((b, m, n), BF16),
        compiler_params=_cparams(("parallel", "arbitrary")),
        name="mem_kv",
    )(mem, mem_norm_w.reshape(1, d), w_kv)


def _norm_matmul_kernel(x_ref, nw_ref, w_ref, ws_ref, out_ref, side_ref, xn_ref):
    @pl.when(pl.program_id(1) == 0)
    def _():
        xn_ref[...] = _rms(x_ref[...], nw_ref[...]).astype(BF16)
        side_ref[...] = _dot(xn_ref[...], ws_ref[...])

    out_ref[...] = _dot(xn_ref[...], w_ref[...]).astype(out_ref.dtype)


def _norm_matmul(x, norm_w, w, w_side, tm, tn):
    n, d = x.shape
    c = w.shape[1]
    cs = w_side.shape[1]
    return pl.pallas_call(
        _norm_matmul_kernel,
        grid=(n // tm, c // tn),
        in_specs=[pl.BlockSpec((tm, d), lambda i, j: (i, 0)),
                  pl.BlockSpec((1, d), lambda i, j: (0, 0)),
                  pl.BlockSpec((d, tn), lambda i, j: (0, j)),
                  pl.BlockSpec((d, cs), lambda i, j: (0, 0))],
        out_specs=[pl.BlockSpec((tm, tn), lambda i, j: (i, j)),
                   pl.BlockSpec((tm, cs), lambda i, j: (i, 0))],
        out_shape=[jax.ShapeDtypeStruct((n, c), BF16),
                   jax.ShapeDtypeStruct((n, cs), F32)],
        scratch_shapes=[pltpu.VMEM((tm, d), BF16)],
        compiler_params=_cparams(("parallel", "arbitrary")),
        name="norm_matmul",
    )(x, norm_w.reshape(1, d), w, w_side)


def _mem_attention(mq, mz, k, v):
    outs = []
    for h in range(MEM_HEADS):
        sl = slice(h * MEM_HEAD_DIM, (h + 1) * MEM_HEAD_DIM)
        q = (mq[:, sl] * (MEM_HEAD_DIM ** -0.5)).astype(BF16)
        s = _dot_nt(q, k[:, sl])
        p = jnp.exp(s - jnp.max(s, axis=-1, keepdims=True))
        o = _dot(p.astype(BF16), v[:, sl]) * (1.0 / jnp.sum(p, axis=-1, keepdims=True))
        outs.append(o * _silu(mz[:, sl]))
    return outs


_O_SCB, _O_SCC, _O_SCV, _O_SCZ = 0, SC_WIDTH, 2 * SC_WIDTH, 3 * SC_WIDTH
_O_SSDZ = 4 * SC_WIDTH
_O_XBC = _O_SSDZ + SSD_WIDTH
_O_MQ0 = _O_XBC + SSD_CONV_DIM
_O_MZ0 = _O_MQ0 + MEM_WIDTH


def _even_mixer_kernel(p_ref, dt_ref, k_ref, v_ref, scw_ref, xw_ref, xb_ref, dtb_ref, alog_ref, d_ref,
                       nw_ref, ehp_ref, ehk_ref, tri_ref, y_ref, cv_buf, xbc_buf, state):
    tile = p_ref.shape[0]

    @pl.when(pl.program_id(1) == 0)
    def _():
        cv_buf[0:HALO, :] = jnp.zeros((HALO, SC_WIDTH), F32)
        xbc_buf[0:HALO, :] = jnp.zeros((HALO, SSD_CONV_DIM), F32)
        state[...] = jnp.zeros(state.shape, F32)

    cv_buf[HALO:HALO + tile, :] = (p_ref[:, _O_SCC:_O_SCC + SC_WIDTH].astype(F32)
                                   * p_ref[:, _O_SCV:_O_SCV + SC_WIDTH].astype(F32))
    conv = None
    for k in range(SC_KERNEL):
        off = HALO - (SC_KERNEL - 1) + k
        t = scw_ref[k:k + 1, :] * cv_buf[off:off + tile, :]
        conv = t if conv is None else conv + t
    y_a = (p_ref[:, _O_SCB:_O_SCB + SC_WIDTH].astype(F32) * conv
           * _silu(p_ref[:, _O_SCZ:_O_SCZ + SC_WIDTH].astype(F32)))
    y_ref[:, 0:SC_WIDTH] = y_a.astype(BF16)
    cv_buf[0:HALO, :] = cv_buf[tile:tile + HALO, :]

    y_m = _mem_attention(p_ref[:, _O_MQ0:_O_MQ0 + MEM_WIDTH].astype(F32),
                         p_ref[:, _O_MZ0:_O_MZ0 + MEM_WIDTH].astype(F32), k_ref[0], v_ref[0])
    for h in range(MEM_HEADS):
        c0 = SC_WIDTH + SSD_WIDTH + h * MEM_HEAD_DIM
        y_ref[:, c0:c0 + MEM_HEAD_DIM] = y_m[h].astype(BF16)

    xbc_buf[HALO:HALO + tile, :] = p_ref[:, _O_XBC:_O_XBC + SSD_CONV_DIM].astype(F32)
    a_row = -jnp.exp(alog_ref[...])
    row = lax.broadcasted_iota(jnp.int32, (SSD_CHUNK, SSD_CHUNK), 0)
    col = lax.broadcasted_iota(jnp.int32, (SSD_CHUNK, SSD_CHUNK), 1)
    causal = row >= col
    first_head = col < SSD_HEAD_DIM
    heads_per_group = SSD_HEADS // SSD_GROUPS
    for c in range(tile // SSD_CHUNK):
        r0 = c * SSD_CHUNK
        u = xb_ref[...]
        for k in range(SSD_CONV):
            off = HALO - (SSD_CONV - 1) + k + r0
            u = u + xw_ref[k:k + 1, :] * xbc_buf[off:off + SSD_CHUNK, :]
        xbc = _silu(u)
        xs = xbc[:, 0:SSD_WIDTH]
        bm = xbc[:, SSD_WIDTH:SSD_WIDTH + SSD_GROUPS * SSD_STATE]
        cm = xbc[:, SSD_WIDTH + SSD_GROUPS * SSD_STATE:]
        dt_in = dt_ref[r0:r0 + SSD_CHUNK, :] + dtb_ref[...]
        dt = jnp.maximum(dt_in, 0.0) + jnp.log1p(jnp.exp(-jnp.abs(dt_in)))
        dta = dt * a_row
        tri = tri_ref[...]
        a_cum = None
        for piece in _split_bf16(dta, 3):
            t = _dot(tri, piece)
            a_cum = t if a_cum is None else a_cum + t
        a_cum_t = a_cum.T
        a_last = a_cum[SSD_CHUNK - 1:SSD_CHUNK, :]
        exp_a = jnp.exp(a_cum)
        dt_e = _expand(dt, ehp_ref, 2)
        exp_a_e = _expand(exp_a, ehp_ref, 2)
        dtd_e = _expand(dt * jnp.exp(a_last - a_cum), ehp_ref, 2)
        a_q = _expand(a_cum, ehk_ref, 3)
        xdt = xs * dt_e
        xdte = xs * dtd_e
        chunk_decay_e = exp_a_e[SSD_CHUNK - 1:SSD_CHUNK, :]
        z = p_ref[r0:r0 + SSD_CHUNK, _O_SSDZ:_O_SSDZ + SSD_WIDTH].astype(F32)
        for g in range(SSD_GROUPS):
            gs = slice(g * SSD_GROUP_WIDTH, (g + 1) * SSD_GROUP_WIDTH)
            bm_g = bm[:, g * SSD_STATE:(g + 1) * SSD_STATE]
            cm_g = cm[:, g * SSD_STATE:(g + 1) * SSD_STATE].astype(BF16)
            cb = _dot_nt(cm_g, bm_g.astype(BF16))
            s_enter = state[g]
            y_off = _dot(cm_g, s_enter.astype(BF16)) * exp_a_e[:, gs]
            state[g] = s_enter * chunk_decay_e[:, gs] + _dot(bm_g.T.astype(BF16), xdte[:, gs].astype(BF16))
            pairs = []
            for j in range(heads_per_group // 2):
                h0 = g * heads_per_group + 2 * j
                lmat = []
                for h in (h0, h0 + 1):
                    seg = a_q[:, h * SSD_CHUNK:(h + 1) * SSD_CHUNK] - a_cum_t[h:h + 1, :]
                    lmat.append((cb * jnp.exp(jnp.where(causal, seg, -jnp.inf))).astype(BF16))
                xp = xdt[:, h0 * SSD_HEAD_DIM:(h0 + 2) * SSD_HEAD_DIM]
                rhs = jnp.concatenate([jnp.where(first_head, xp, 0.0), jnp.where(first_head, 0.0, xp)],
                                      axis=0).astype(BF16)
                pairs.append(_dot(jnp.concatenate(lmat, axis=1), rhs))
            y_g = (jnp.concatenate(pairs, axis=1) + y_off + xs[:, gs] * d_ref[:, gs]) * _silu(z[:, gs])
            y_g = y_g * lax.rsqrt(jnp.mean(y_g * y_g, axis=-1, keepdims=True) + EPS) * nw_ref[:, gs]
            c0 = SC_WIDTH + g * SSD_GROUP_WIDTH
            y_ref[r0:r0 + SSD_CHUNK, c0:c0 + SSD_GROUP_WIDTH] = y_g.astype(BF16)
    xbc_buf[0:HALO, :] = xbc_buf[tile:tile + HALO, :]


def _even_mixer(proj, dt_raw, mem_kv, sc_conv_w, ssd_conv_w, ssd_conv_b, dt_bias, a_log, d_e, ssd_norm_w,
                e_hp, e_hk, tri, batch, tile):
    n = proj.shape[0]
    tiles = n // batch // tile
    const = lambda shape: pl.BlockSpec(shape, lambda b, t: (0,) * len(shape))
    return pl.pallas_call(
        _even_mixer_kernel,
        grid=(batch, tiles),
        in_specs=[pl.BlockSpec((tile, EVEN_MAIN), lambda b, t: (b * tiles + t, 0)),
                  pl.BlockSpec((tile, LANES), lambda b, t: (b * tiles + t, 0)),
                  pl.BlockSpec((1, N_MEM, MEM_WIDTH), lambda b, t: (b, 0, 0)),
                  pl.BlockSpec((1, N_MEM, MEM_WIDTH), lambda b, t: (b, 0, 1)),
                  const((SC_KERNEL, SC_WIDTH)), const((SSD_CONV, SSD_CONV_DIM)), const((1, SSD_CONV_DIM)),
                  const((1, LANES)), const((1, LANES)), const((1, SSD_WIDTH)), const((1, SSD_WIDTH)),
                  const((LANES, SSD_WIDTH)), const((LANES, SSD_HEADS * SSD_CHUNK)),
                  const((SSD_CHUNK, SSD_CHUNK))],
        out_specs=pl.BlockSpec((tile, EVEN_OUT), lambda b, t: (b * tiles + t, 0)),
        out_shape=jax.ShapeDtypeStruct((n, EVEN_OUT), BF16),
        scratch_shapes=[pltpu.VMEM((tile + HALO, SC_WIDTH), F32),
                        pltpu.VMEM((tile + HALO, SSD_CONV_DIM), F32),
                        pltpu.VMEM((SSD_GROUPS, SSD_STATE, SSD_GROUP_WIDTH), F32)],
        compiler_params=_cparams(("parallel", "arbitrary")),
        name="even_mixer",
    )(proj, dt_raw, mem_kv, mem_kv, sc_conv_w, ssd_conv_w, ssd_conv_b, dt_bias, a_log, d_e, ssd_norm_w,
      e_hp, e_hk, tri)


def _residual_matmul_kernel(y_ref, w_ref, x_ref, out_ref):
    out_ref[...] = x_ref[...] + _dot(y_ref[...], w_ref[...])


def _residual_matmul(y, w, x, tm, tn):
    n, k = y.shape
    d = w.shape[1]
    return pl.pallas_call(
        _residual_matmul_kernel,
        grid=(n // tm, d // tn),
        in_specs=[pl.BlockSpec((tm, k), lambda i, j: (i, 0)),
                  pl.BlockSpec((k, tn), lambda i, j: (0, j)),
                  pl.BlockSpec((tm, tn), lambda i, j: (i, j))],
        out_specs=pl.BlockSpec((tm, tn), lambda i, j: (i, j)),
        out_shape=jax.ShapeDtypeStruct((n, d), F32),
        compiler_params=_cparams(("parallel", "arbitrary")),
        name="residual_matmul",
    )(y, w, x)


_O_CQ, _O_CKV = 0, MLA_Q_RANK
_O_ZC = MLA_Q_RANK + MLA_KV_RANK
_O_MQ1 = _O_ZC + MLA_WIDTH
_O_MZ1 = _O_MQ1 + MEM_WIDTH


def _rope_lanes(a, cos_t, sin_t, low_half):
    half = MLA_ROPE // 2
    swapped = jnp.where(low_half, pltpu.roll(a, LANES - half, 1), pltpu.roll(a, half, 1))
    return a * cos_t + swapped * sin_t


def _mla_qkv_kernel(p_ref, kr_ref, pos_ref, qnw_ref, kvnw_ref, wq_ref, wkv_ref, freq_ref, sign_ref,
                    q_ref, k_ref, v_ref):
    tile = p_ref.shape[0]
    cq = _rms(p_ref[:, _O_CQ:_O_CQ + MLA_Q_RANK].astype(F32), qnw_ref[...])
    q = _dot(cq.astype(BF16), wq_ref[...])
    ckv = _rms(p_ref[:, _O_CKV:_O_CKV + MLA_KV_RANK].astype(F32), kvnw_ref[...])
    kv = _dot(ckv.astype(BF16), wkv_ref[...])
    ang = pos_ref[...].astype(F32) * freq_ref[...]
    cos_t = jnp.cos(ang)
    sin_t = jnp.sin(ang) * sign_ref[...]
    low_half = lax.broadcasted_iota(jnp.int32, (tile, LANES), 1) < MLA_ROPE // 2
    k_rope = _rope_lanes(kr_ref[...], cos_t, sin_t, low_half).astype(BF16)
    scale = (MLA_NOPE + MLA_ROPE) ** -0.5 * np.log2(np.e)
    ones = jnp.ones((tile, MLA_V), BF16)
    for h in range(MLA_HEADS):
        nope = slice(h * MLA_NOPE, (h + 1) * MLA_NOPE)
        rope = slice(MLA_WIDTH + h * LANES, MLA_WIDTH + (h + 1) * LANES)
        q_ref[0, h, :, 0:MLA_NOPE] = (q[:, nope] * scale).astype(BF16)
        q_ref[0, h, :, MLA_NOPE:MLA_QK_PAD] = (_rope_lanes(q[:, rope], cos_t, sin_t, low_half) * scale).astype(BF16)
        k_ref[0, h, :, 0:MLA_NOPE] = kv[:, nope].astype(BF16)
        k_ref[0, h, :, MLA_NOPE:MLA_QK_PAD] = k_rope
        v_ref[0, h, :, 0:MLA_V] = kv[:, MLA_WIDTH + h * MLA_V:MLA_WIDTH + (h + 1) * MLA_V].astype(BF16)
        v_ref[0, h, :, MLA_V:2 * MLA_V] = ones


def _mla_qkv(proj, kr, pos, q_norm_w, kv_norm_w, w_uq, w_ukv, freq, sign, batch, tile):
    n = proj.shape[0]
    seq = n // batch
    tiles = seq // tile
    const = lambda shape: pl.BlockSpec(shape, lambda b, t: (0,) * len(shape))
    head_spec = lambda w: pl.BlockSpec((1, MLA_HEADS, tile, w), lambda b, t: (b, 0, t, 0))
    return pl.pallas_call(
        _mla_qkv_kernel,
        grid=(batch, tiles),
        in_specs=[pl.BlockSpec((tile, MLA_Q_RANK + MLA_KV_RANK), lambda b, t: (b * tiles + t, 0)),
                  pl.BlockSpec((tile, LANES), lambda b, t: (b * tiles + t, 0)),
                  pl.BlockSpec((tile, 1), lambda b, t: (b * tiles + t, 0)),
                  const((1, MLA_Q_RANK)), const((1, MLA_KV_RANK)),
                  const((MLA_Q_RANK, MLA_WIDTH + MLA_HEADS * LANES)), const((MLA_KV_RANK, 2 * MLA_WIDTH)),
                  const((1, LANES)), const((1, LANES))],
        out_specs=[head_spec(MLA_QK_PAD), head_spec(MLA_QK_PAD), head_spec(2 * MLA_V)],
        out_shape=[jax.ShapeDtypeStruct((batch, MLA_HEADS, seq, MLA_QK_PAD), BF16),
                   jax.ShapeDtypeStruct((batch, MLA_HEADS, seq, MLA_QK_PAD), BF16),
                   jax.ShapeDtypeStruct((batch, MLA_HEADS, seq, 2 * MLA_V), BF16)],
        compiler_params=_cparams(("parallel", "parallel")),
        name="mla_qkv",
    )(proj, kr, pos, q_norm_w, kv_norm_w, w_uq, w_ukv, freq, sign)


def _causal_attention_kernel(q_ref, k_ref, v_ref, o_ref, s_ref, m_ref, acc_ref):
    chains = 2
    tk = q_ref.shape[2] // chains
    qi = pl.program_id(2)
    m_ref[...] = jnp.full(m_ref.shape, -1e30, F32)
    acc_ref[...] = jnp.zeros(acc_ref.shape, F32)

    def kv_rows(j):
        return pl.ds(pl.multiple_of(j * tk, tk), tk)

    def scores(c, j, buf):
        s_ref[buf, c] = _dot_nt(q_ref[0, 0, c * tk:(c + 1) * tk, :], k_ref[0, 0, kv_rows(j), :])

    def softmax_pv(c, j, buf, diagonal):
        s = s_ref[buf, c]
        if diagonal:
            row = lax.broadcasted_iota(jnp.int32, (tk, tk), 0)
            col = lax.broadcasted_iota(jnp.int32, (tk, tk), 1)
            s = jnp.where(col <= row, s, -1e30)
        m = m_ref[c]
        m_new = jnp.maximum(m, jnp.max(s, axis=-1, keepdims=True))
        m_ref[c] = m_new
        p = jnp.exp2(s - jnp.concatenate([m_new] * (tk // LANES), axis=1)).astype(BF16)
        alpha = jnp.exp2(m - m_new)
        pv = _dot(p, v_ref[0, 0, kv_rows(j), :])
        for half in range(2):
            lanes = slice(half * MLA_V, (half + 1) * MLA_V)
            acc_ref[c, :, lanes] = alpha * acc_ref[c, :, lanes] + pv[:, lanes]

    for c in range(chains):
        scores(c, 0, 0)

    def pair(jj, carry):
        j = 2 * jj
        for cur in range(2):
            for c in range(chains):
                scores(c, j + cur + 1, 1 - cur)
            for c in range(chains):
                softmax_pv(c, j + cur, cur, False)
        return carry

    lax.fori_loop(0, qi, pair, 0)
    j = 2 * qi
    scores(1, j + 1, 1)
    softmax_pv(0, j, 0, True)
    softmax_pv(1, j, 0, False)
    softmax_pv(1, j + 1, 1, True)
    for c in range(chains):
        o_ref[c * tk:(c + 1) * tk, :] = (acc_ref[c, :, 0:MLA_V] / acc_ref[c, :, MLA_V:]).astype(o_ref.dtype)


def _causal_attention(q, k, v, tq):
    batch, heads, seq, _ = q.shape
    nq = seq // tq
    tk = tq // 2
    return pl.pallas_call(
        _causal_attention_kernel,
        grid=(batch, heads, nq),
        in_specs=[pl.BlockSpec((1, 1, tq, MLA_QK_PAD), lambda b, h, i: (b, h, i, 0)),
                  pl.BlockSpec((1, 1, seq, MLA_QK_PAD), lambda b, h, i: (b, h, 0, 0)),
                  pl.BlockSpec((1, 1, seq, 2 * MLA_V), lambda b, h, i: (b, h, 0, 0))],
        out_specs=pl.BlockSpec((tq, MLA_V), lambda b, h, i: (b * nq + i, h)),
        out_shape=jax.ShapeDtypeStruct((batch * seq, heads * MLA_V), BF16),
        scratch_shapes=[pltpu.VMEM((2, 2, tk, tk), F32),
                        pltpu.VMEM((2, tk, LANES), F32),
                        pltpu.VMEM((2, tk, 2 * MLA_V), F32)],
        compiler_params=_cparams(("parallel", "parallel", "arbitrary")),
        name="causal_attention",
    )(q, k, v)


def _odd_tail_kernel(a_ref, p_ref, k_ref, v_ref, x_ref, w_ref, fw_ref, out_ref):
    y_c = a_ref[...].astype(F32) * _silu(p_ref[:, _O_ZC:_O_ZC + MLA_WIDTH].astype(F32))
    acc = x_ref[...] + _dot(y_c.astype(BF16), w_ref[0:MLA_WIDTH, :])
    y_m = _mem_attention(p_ref[:, _O_MQ1:_O_MQ1 + MEM_WIDTH].astype(F32),
                         p_ref[:, _O_MZ1:_O_MZ1 + MEM_WIDTH].astype(F32), k_ref[0], v_ref[0])
    y_m = jnp.concatenate(y_m, axis=1).astype(BF16)
    acc = acc + _dot(y_m, w_ref[MLA_WIDTH:ODD_OUT, :])
    out_ref[...] = _rms(acc, fw_ref[...])


def _odd_tail(attn, proj, mem_kv, x, w_out, final_w, batch, tile):
    n = x.shape[0]
    tiles = n // batch // tile
    const = lambda shape: pl.BlockSpec(shape, lambda b, t: (0,) * len(shape))
    return pl.pallas_call(
        _odd_tail_kernel,
        grid=(batch, tiles),
        in_specs=[pl.BlockSpec((tile, MLA_WIDTH), lambda b, t: (b * tiles + t, 0)),
                  pl.BlockSpec((tile, ODD_MAIN), lambda b, t: (b * tiles + t, 0)),
                  pl.BlockSpec((1, N_MEM, MEM_WIDTH), lambda b, t: (b, 0, 2)),
                  pl.BlockSpec((1, N_MEM, MEM_WIDTH), lambda b, t: (b, 0, 3)),
                  pl.BlockSpec((tile, D_MODEL), lambda b, t: (b * tiles + t, 0)),
                  const((ODD_OUT, D_MODEL)), const((1, D_MODEL))],
        out_specs=pl.BlockSpec((tile, D_MODEL), lambda b, t: (b * tiles + t, 0)),
        out_shape=jax.ShapeDtypeStruct((n, D_MODEL), F32),
        compiler_params=_cparams(("parallel", "parallel")),
        name="odd_tail",
    )(attn, proj, mem_kv, mem_kv, x, w_out, final_w)


def _pad_cols(w, width):
    return jnp.pad(w, ((0, 0), (0, width - w.shape[1])))


def _expansion_matrix(width_per_head):
    e = np.zeros((LANES, SSD_HEADS * width_per_head), np.float32)
    for h in range(SSD_HEADS):
        e[h, h * width_per_head:(h + 1) * width_per_head] = 1.0
    return jnp.asarray(e, BF16)


def kernel(x, mem, positions, mem_norm_w, norm0_w, w_in0, sc_conv_w, ssd_conv_w, ssd_conv_b, ssd_dt_bias,
           ssd_a_log, ssd_d, ssd_norm_w, mem_k0, mem_v0, w_out0, norm1_w, w_in1, mla_q_norm_w,
           mla_kv_norm_w, mla_w_uq, mla_w_ukv, mem_k1, mem_v1, w_out1, final_norm_w):
    batch, seq, d = x.shape
    n = batch * seq
    x2d = x.reshape(n, d)
    row = lambda v: v.reshape(1, -1).astype(F32)

    mem_kv = _mem_kv(mem, mem_norm_w, jnp.concatenate([mem_k0, mem_v0, mem_k1, mem_v1], axis=1).astype(BF16))

    o_dt = _O_MQ0
    w0_main = jnp.concatenate([w_in0[:, :o_dt], w_in0[:, o_dt + SSD_HEADS:]], axis=1).astype(BF16)
    w0_dt = _pad_cols(w_in0[:, o_dt:o_dt + SSD_HEADS], LANES).astype(BF16)
    proj0, dt_raw = _norm_matmul(x2d, norm0_w, w0_main, w0_dt, tm=min(1024, n), tn=1024)
    pad_heads = lambda v: _pad_cols(row(v), LANES)
    y0 = _even_mixer(proj0, dt_raw, mem_kv, sc_conv_w, ssd_conv_w, row(ssd_conv_b), pad_heads(ssd_dt_bias),
                     pad_heads(ssd_a_log), row(jnp.repeat(ssd_d, SSD_HEAD_DIM)), row(ssd_norm_w),
                     _expansion_matrix(SSD_HEAD_DIM), _expansion_matrix(SSD_CHUNK),
                     jnp.asarray(np.tril(np.ones((SSD_CHUNK, SSD_CHUNK), np.float32)), BF16),
                     batch, tile=min(256, seq))
    x1 = _residual_matmul(y0, w_out0.astype(BF16), x2d, tm=min(512, n), tn=1024)

    o_kr = MLA_Q_RANK + MLA_KV_RANK
    w1_main = jnp.concatenate([w_in1[:, :o_kr], w_in1[:, o_kr + MLA_ROPE:]], axis=1).astype(BF16)
    w1_kr = _pad_cols(w_in1[:, o_kr:o_kr + MLA_ROPE], LANES).astype(BF16)
    proj1, kr = _norm_matmul(x1, norm1_w, w1_main, w1_kr, tm=min(512, n), tn=ODD_MAIN // 2)
    wq = mla_w_uq.reshape(MLA_Q_RANK, MLA_HEADS, MLA_NOPE + MLA_ROPE)
    wq_rope = jnp.pad(wq[:, :, MLA_NOPE:], ((0, 0), (0, 0), (0, LANES - MLA_ROPE)))
    wq = jnp.concatenate([wq[:, :, :MLA_NOPE].reshape(MLA_Q_RANK, MLA_WIDTH),
                          wq_rope.reshape(MLA_Q_RANK, MLA_HEADS * LANES)], axis=1).astype(BF16)
    wkv = mla_w_ukv.reshape(MLA_KV_RANK, MLA_HEADS, MLA_NOPE + MLA_V)
    wkv = jnp.concatenate([wkv[:, :, :MLA_NOPE].reshape(MLA_KV_RANK, MLA_WIDTH),
                           wkv[:, :, MLA_NOPE:].reshape(MLA_KV_RANK, MLA_WIDTH)], axis=1).astype(BF16)
    half = MLA_ROPE // 2
    inv = ROPE_THETA ** (-jnp.arange(half, dtype=F32) / half)
    freq = jnp.concatenate([inv, inv, jnp.zeros((LANES - MLA_ROPE,), F32)]).reshape(1, LANES)
    sign = np.zeros((1, LANES), np.float32)
    sign[0, :half] = -1.0
    sign[0, half:MLA_ROPE] = 1.0
    q, k, v = _mla_qkv(proj1, kr, positions.reshape(n, 1), row(mla_q_norm_w), row(mla_kv_norm_w), wq, wkv,
                       freq, jnp.asarray(sign), batch, tile=min(512, seq))
    attn = _causal_attention(q, k, v, tq=min(1024, seq))
    out = _odd_tail(attn, proj1, mem_kv, x1, w_out1.astype(BF16), row(final_norm_w), batch, tile=min(256, seq))
    return out.reshape(batch, seq, d)
```

```python
import functools

import numpy as np
import jax
import jax.numpy as jnp
from jax import lax
from jax.experimental import pallas as pl
from jax.experimental.pallas import tpu as pltpu

F32 = jnp.float32
BF16 = jnp.bfloat16

D_MODEL = 2048
N_MEM = 256
EPS = 1e-6
SC_WIDTH = 1024
SC_KERNEL = 3
SSD_HEADS = 32
SSD_HEAD_DIM = 64
SSD_WIDTH = SSD_HEADS * SSD_HEAD_DIM
SSD_STATE = 128
SSD_GROUPS = 4
SSD_CONV = 4
SSD_CHUNK = 128
SSD_CONV_DIM = SSD_WIDTH + 2 * SSD_GROUPS * SSD_STATE
SSD_GROUP_WIDTH = SSD_WIDTH // SSD_GROUPS
MLA_HEADS = 16
MLA_Q_RANK = 768
MLA_KV_RANK = 512
MLA_NOPE = 128
MLA_ROPE = 64
MLA_V = 128
MLA_WIDTH = MLA_HEADS * MLA_V
MLA_QK_PAD = 256
ROPE_THETA = 10000.0
MEM_HEADS = 4
MEM_HEAD_DIM = 128
MEM_WIDTH = MEM_HEADS * MEM_HEAD_DIM
EVEN_MAIN = 4 * SC_WIDTH + SSD_WIDTH + SSD_CONV_DIM + 2 * MEM_WIDTH
EVEN_OUT = SC_WIDTH + SSD_WIDTH + MEM_WIDTH
ODD_MAIN = MLA_Q_RANK + MLA_KV_RANK + MLA_WIDTH + 2 * MEM_WIDTH
ODD_OUT = MLA_WIDTH + MEM_WIDTH
LANES = 128
HALO = 8

VMEM_LIMIT = 56 * 1024 * 1024


def _cparams(sem):
    return pltpu.CompilerParams(dimension_semantics=sem, vmem_limit_bytes=VMEM_LIMIT)


def _dot(a, b):
    return jnp.dot(a, b, preferred_element_type=F32)


def _dot_nt(a, b):
    return lax.dot_general(a, b, (((1,), (1,)), ((), ())), preferred_element_type=F32)


def _rms(x, w):
    return x * lax.rsqrt(jnp.mean(x * x, axis=-1, keepdims=True) + EPS) * w


def _silu(z):
    h = 0.5 * z
    return h + h * jnp.tanh(h)


def _split_bf16(v, n):
    parts = []
    r = v
    for _ in range(n):
        p = r.astype(BF16)
        parts.append(p)
        r = r - p.astype(F32)
    return parts


def _expand(v, e_ref, n):
    e = e_ref[...]
    out = None
    for p in _split_bf16(v, n):
        t = _dot(p, e)
        out = t if out is None else out + t
    return out


def _mem_kv_kernel(mem_ref, nw_ref, w_ref, out_ref):
    m = _rms(mem_ref[0], nw_ref[...])
    out_ref[0] = _dot(m.astype(BF16), w_ref[...]).astype(BF16)


def _mem_kv(mem, mem_norm_w, w_kv):
    b, m, d = mem.shape
    n = w_kv.shape[1]
    tn = MEM_WIDTH
    return pl.pallas_call(
        _mem_kv_kernel,
        grid=(b, n // tn),
        in_specs=[pl.BlockSpec((1, m, d), lambda i, j: (i, 0, 0)),
                  pl.BlockSpec((1, d), lambda i, j: (0, 0)),
                  pl.BlockSpec((d, tn), lambda i, j: (0, j))],
        out_specs=pl.BlockSpec((1, m, tn), lambda i, j: (i, 0, j)),
        out_shape=jax.ShapeDtypeStruct((b, m, n), BF16),
        compiler_params=_cparams(("parallel", "arbitrary")),
        name="mem_kv",
    )(mem, mem_norm_w.reshape(1, d), w_kv)


def _norm_matmul_kernel(x_ref, nw_ref, w_ref, ws_ref, out_ref, side_ref, xn_ref):
    @pl.when(pl.program_id(1) == 0)
    def _():
        xn_ref[...] = _rms(x_ref[...], nw_ref[...]).astype(BF16)
        side_ref[...] = _dot(xn_ref[...], ws_ref[...])

    out_ref[...] = _dot(xn_ref[...], w_ref[...]).astype(out_ref.dtype)


def _norm_matmul(x, norm_w, w, w_side, tm, tn):
    n, d = x.shape
    c = w.shape[1]
    cs = w_side.shape[1]
    return pl.pallas_call(
        _norm_matmul_kernel,
        grid=(n // tm, c // tn),
        in_specs=[pl.BlockSpec((tm, d), lambda i, j: (i, 0)),
                  pl.BlockSpec((1, d), lambda i, j: (0, 0)),
                  pl.BlockSpec((d, tn), lambda i, j: (0, j)),
                  pl.BlockSpec((d, cs), lambda i, j: (0, 0))],
        out_specs=[pl.BlockSpec((tm, tn), lambda i, j: (i, j)),
                   pl.BlockSpec((tm, cs), lambda i, j: (i, 0))],
        out_shape=[jax.ShapeDtypeStruct((n, c), BF16),
                   jax.ShapeDtypeStruct((n, cs), F32)],
        scratch_shapes=[pltpu.VMEM((tm, d), BF16)],
        compiler_params=_cparams(("parallel", "arbitrary")),
        name="norm_matmul",
    )(x, norm_w.reshape(1, d), w, w_side)


def _mem_attention(mq, mz, k, v):
    outs = []
    for h in range(MEM_HEADS):
        sl = slice(h * MEM_HEAD_DIM, (h + 1) * MEM_HEAD_DIM)
        q = (mq[:, sl] * (MEM_HEAD_DIM ** -0.5)).astype(BF16)
        s = _dot_nt(q, k[:, sl])
        p = jnp.exp(s - jnp.max(s, axis=-1, keepdims=True))
        o = _dot(p.astype(BF16), v[:, sl]) * (1.0 / jnp.sum(p, axis=-1, keepdims=True))
        outs.append(o * _silu(mz[:, sl]))
    return outs


_O_SCB, _O_SCC, _O_SCV, _O_SCZ = 0, SC_WIDTH, 2 * SC_WIDTH, 3 * SC_WIDTH
_O_SSDZ = 4 * SC_WIDTH
_O_XBC = _O_SSDZ + SSD_WIDTH
_O_MQ0 = _O_XBC + SSD_CONV_DIM
_O_MZ0 = _O_MQ0 + MEM_WIDTH


def _even_mixer_kernel(p_ref, dt_ref, k_ref, v_ref, scw_ref, xw_ref, xb_ref, dtb_ref, alog_ref, d_ref,
                       nw_ref, ehp_ref, ehk_ref, tri_ref, y_ref, cv_buf, xbc_buf, state):
    tile = p_ref.shape[0]

    @pl.when(pl.program_id(1) == 0)
    def _():
        cv_buf[0:HALO, :] = jnp.zeros((HALO, SC_WIDTH), F32)
        xbc_buf[0:HALO, :] = jnp.zeros((HALO, SSD_CONV_DIM), F32)
        state[...] = jnp.zeros(state.shape, F32)

    cv_buf[HALO:HALO + tile, :] = (p_ref[:, _O_SCC:_O_SCC + SC_WIDTH].astype(F32)
                                   * p_ref[:, _O_SCV:_O_SCV + SC_WIDTH].astype(F32))
    conv = None
    for k in range(SC_KERNEL):
        off = HALO - (SC_KERNEL - 1) + k
        t = scw_ref[k:k + 1, :] * cv_buf[off:off + tile, :]
        conv = t if conv is None else conv + t
    y_a = (p_ref[:, _O_SCB:_O_SCB + SC_WIDTH].astype(F32) * conv
           * _silu(p_ref[:, _O_SCZ:_O_SCZ + SC_WIDTH].astype(F32)))
    y_ref[:, 0:SC_WIDTH] = y_a.astype(BF16)
    cv_buf[0:HALO, :] = cv_buf[tile:tile + HALO, :]

    y_m = _mem_attention(p_ref[:, _O_MQ0:_O_MQ0 + MEM_WIDTH].astype(F32),
                         p_ref[:, _O_MZ0:_O_MZ0 + MEM_WIDTH].astype(F32), k_ref[0], v_ref[0])
    for h in range(MEM_HEADS):
        c0 = SC_WIDTH + SSD_WIDTH + h * MEM_HEAD_DIM
        y_ref[:, c0:c0 + MEM_HEAD_DIM] = y_m[h].astype(BF16)

    xbc_buf[HALO:HALO + tile, :] = p_ref[:, _O_XBC:_O_XBC + SSD_CONV_DIM].astype(F32)
    a_row = -jnp.exp(alog_ref[...])
    row = lax.broadcasted_iota(jnp.int32, (SSD_CHUNK, SSD_CHUNK), 0)
    col = lax.broadcasted_iota(jnp.int32, (SSD_CHUNK, SSD_CHUNK), 1)
    causal = row >= col
    first_head = col < SSD_HEAD_DIM
    heads_per_group = SSD_HEADS // SSD_GROUPS
    for c in range(tile // SSD_CHUNK):
        r0 = c * SSD_CHUNK
        u = xb_ref[...]
        for k in range(SSD_CONV):
            off = HALO - (SSD_CONV - 1) + k + r0
            u = u + xw_ref[k:k + 1, :] * xbc_buf[off:off + SSD_CHUNK, :]
        xbc = _silu(u)
        xs = xbc[:, 0:SSD_WIDTH]
        bm = xbc[:, SSD_WIDTH:SSD_WIDTH + SSD_GROUPS * SSD_STATE]
        cm = xbc[:, SSD_WIDTH + SSD_GROUPS * SSD_STATE:]
        dt_in = dt_ref[r0:r0 + SSD_CHUNK, :] + dtb_ref[...]
        dt = jnp.maximum(dt_in, 0.0) + jnp.log1p(jnp.exp(-jnp.abs(dt_in)))
        dta = dt * a_row
        tri = tri_ref[...]
        a_cum = None
        for piece in _split_bf16(dta, 3):
            t = _dot(tri, piece)
            a_cum = t if a_cum is None else a_cum + t
        a_cum_t = a_cum.T
        a_last = a_cum[SSD_CHUNK - 1:SSD_CHUNK, :]
        exp_a = jnp.exp(a_cum)
        dt_e = _expand(dt, ehp_ref, 2)
        exp_a_e = _expand(exp_a, ehp_ref, 2)
        dtd_e = _expand(dt * jnp.exp(a_last - a_cum), ehp_ref, 2)
        a_q = _expand(a_cum, ehk_ref, 3)
        xdt = xs * dt_e
        xdte = xs * dtd_e
        chunk_decay_e = exp_a_e[SSD_CHUNK - 1:SSD_CHUNK, :]
        z = p_ref[r0:r0 + SSD_CHUNK, _O_SSDZ:_O_SSDZ + SSD_WIDTH].astype(F32)
        for g in range(SSD_GROUPS):
            gs = slice(g * SSD_GROUP_WIDTH, (g + 1) * SSD_GROUP_WIDTH)
            bm_g = bm[:, g * SSD_STATE:(g + 1) * SSD_STATE]
            cm_g = cm[:, g * SSD_STATE:(g + 1) * SSD_STATE].astype(BF16)
            cb = _dot_nt(cm_g, bm_g.astype(BF16))
            s_enter = state[g]
            y_off = _dot(cm_g, s_enter.astype(BF16)) * exp_a_e[:, gs]
            state[g] = s_enter * chunk_decay_e[:, gs] + _dot(bm_g.T.astype(BF16), xdte[:, gs].astype(BF16))
            pairs = []
            for j in range(heads_per_group // 2):
                h0 = g * heads_per_group + 2 * j
                lmat = []
                for h in (h0, h0 + 1):
                    seg = a_q[:, h * SSD_CHUNK:(h + 1) * SSD_CHUNK] - a_cum_t[h:h + 1, :]
                    lmat.append((cb * jnp.exp(jnp.where(causal, seg, -jnp.inf))).astype(BF16))
                xp = xdt[:, h0 * SSD_HEAD_DIM:(h0 + 2) * SSD_HEAD_DIM]
                rhs = jnp.concatenate([jnp.where(first_head, xp, 0.0), jnp.where(first_head, 0.0, xp)],
                                      axis=0).astype(BF16)
                pairs.append(_dot(jnp.concatenate(lmat, axis=1), rhs))
            y_g = (jnp.concatenate(pairs, axis=1) + y_off + xs[:, gs] * d_ref[:, gs]) * _silu(z[:, gs])
            y_g = y_g * lax.rsqrt(jnp.mean(y_g * y_g, axis=-1, keepdims=True) + EPS) * nw_ref[:, gs]
            c0 = SC_WIDTH + g * SSD_GROUP_WIDTH
            y_ref[r0:r0 + SSD_CHUNK, c0:c0 + SSD_GROUP_WIDTH] = y_g.astype(BF16)
    xbc_buf[0:HALO, :] = xbc_buf[tile:tile + HALO, :]


def _even_mixer(proj, dt_raw, mem_kv, sc_conv_w, ssd_conv_w, ssd_conv_b, dt_bias, a_log, d_e, ssd_norm_w,
                e_hp, e_hk, tri, batch, tile):
    n = proj.shape[0]
    tiles = n // batch // tile
    const = lambda shape: pl.BlockSpec(shape, lambda b, t: (0,) * len(shape))
    return pl.pallas_call(
        _even_mixer_kernel,
        grid=(batch, tiles),
        in_specs=[pl.BlockSpec((tile, EVEN_MAIN), lambda b, t: (b * tiles + t, 0)),
                  pl.BlockSpec((tile, LANES), lambda b, t: (b * tiles + t, 0)),
                  pl.BlockSpec((1, N_MEM, MEM_WIDTH), lambda b, t: (b, 0, 0)),
                  pl.BlockSpec((1, N_MEM, MEM_WIDTH), lambda b, t: (b, 0, 1)),
                  const((SC_KERNEL, SC_WIDTH)), const((SSD_CONV, SSD_CONV_DIM)), const((1, SSD_CONV_DIM)),
                  const((1, LANES)), const((1, LANES)), const((1, SSD_WIDTH)), const((1, SSD_WIDTH)),
                  const((LANES, SSD_WIDTH)), const((LANES, SSD_HEADS * SSD_CHUNK)),
                  const((SSD_CHUNK, SSD_CHUNK))],
        out_specs=pl.BlockSpec((tile, EVEN_OUT), lambda b, t: (b * tiles + t, 0)),
        out_shape=jax.ShapeDtypeStruct((n, EVEN_OUT), BF16),
        scratch_shapes=[pltpu.VMEM((tile + HALO, SC_WIDTH), F32),
                        pltpu.VMEM((tile + HALO, SSD_CONV_DIM), F32),
                        pltpu.VMEM((SSD_GROUPS, SSD_STATE, SSD_GROUP_WIDTH), F32)],
        compiler_params=_cparams(("parallel", "arbitrary")),
        name="even_mixer",
    )(proj, dt_raw, mem_kv, mem_kv, sc_conv_w, ssd_conv_w, ssd_conv_b, dt_bias, a_log, d_e, ssd_norm_w,
      e_hp, e_hk, tri)


def _residual_matmul_kernel(y_ref, w_ref, x_ref, out_ref):
    out_ref[...] = x_ref[...] + _dot(y_ref[...], w_ref[...])


def _resident(shape):
    return pl.BlockSpec(shape, lambda *_: (0,) * len(shape), pipeline_mode=pl.Buffered(1))


def _residual_matmul(y, w, x, tm):
    n, k = y.shape
    d = w.shape[1]
    return pl.pallas_call(
        _residual_matmul_kernel,
        grid=(n // tm,),
        in_specs=[pl.BlockSpec((tm, k), lambda i: (i, 0)),
                  _resident((k, d)),
                  pl.BlockSpec((tm, d), lambda i: (i, 0))],
        out_specs=pl.BlockSpec((tm, d), lambda i: (i, 0)),
        out_shape=jax.ShapeDtypeStruct((n, d), F32),
        compiler_params=_cparams(("parallel",)),
        name="residual_matmul",
    )(y, w, x)


_O_CQ, _O_CKV = 0, MLA_Q_RANK
_O_ZC = MLA_Q_RANK + MLA_KV_RANK
_O_MQ1 = _O_ZC + MLA_WIDTH
_O_MZ1 = _O_MQ1 + MEM_WIDTH


def _rope_lanes(a, cos_t, sin_t, low_half):
    half = MLA_ROPE // 2
    swapped = jnp.where(low_half, pltpu.roll(a, LANES - half, 1), pltpu.roll(a, half, 1))
    return a * cos_t + swapped * sin_t


def _mla_qkv_kernel(p_ref, kr_ref, pos_ref, qnw_ref, kvnw_ref, wq_ref, wkv_ref, freq_ref, sign_ref,
                    q_ref, k_ref, kro_ref, v_ref):
    tile = p_ref.shape[0]
    cq = _rms(p_ref[:, _O_CQ:_O_CQ + MLA_Q_RANK].astype(F32), qnw_ref[...])
    q = _dot(cq.astype(BF16), wq_ref[...])
    ckv = _rms(p_ref[:, _O_CKV:_O_CKV + MLA_KV_RANK].astype(F32), kvnw_ref[...])
    kv = _dot(ckv.astype(BF16), wkv_ref[...])
    ang = pos_ref[...].astype(F32) * freq_ref[...]
    cos_t = jnp.cos(ang)
    sin_t = jnp.sin(ang) * sign_ref[...]
    low_half = lax.broadcasted_iota(jnp.int32, (tile, LANES), 1) < MLA_ROPE // 2
    kro_ref[0] = _rope_lanes(kr_ref[...], cos_t, sin_t, low_half).astype(BF16)
    scale = (MLA_NOPE + MLA_ROPE) ** -0.5 * np.log2(np.e)
    for h in range(MLA_HEADS):
        nope = slice(h * MLA_NOPE, (h + 1) * MLA_NOPE)
        rope = slice(MLA_WIDTH + h * LANES, MLA_WIDTH + (h + 1) * LANES)
        q_ref[0, h, :, 0:MLA_NOPE] = (q[:, nope] * scale).astype(BF16)
        q_ref[0, h, :, MLA_NOPE:MLA_QK_PAD] = (_rope_lanes(q[:, rope], cos_t, sin_t, low_half) * scale).astype(BF16)
        k_ref[0, h] = kv[:, nope].astype(BF16)
        v_ref[0, h] = kv[:, MLA_WIDTH + h * MLA_V:MLA_WIDTH + (h + 1) * MLA_V].astype(BF16)


def _mla_qkv(proj, kr, pos, q_norm_w, kv_norm_w, w_uq, w_ukv, freq, sign, batch, tile):
    n = proj.shape[0]
    seq = n // batch
    tiles = seq // tile
    const = lambda shape: pl.BlockSpec(shape, lambda b, t: (0,) * len(shape))
    head_spec = lambda w: pl.BlockSpec((1, MLA_HEADS, tile, w), lambda b, t: (b, 0, t, 0))
    return pl.pallas_call(
        _mla_qkv_kernel,
        grid=(batch, tiles),
        in_specs=[pl.BlockSpec((tile, MLA_Q_RANK + MLA_KV_RANK), lambda b, t: (b * tiles + t, 0)),
                  pl.BlockSpec((tile, LANES), lambda b, t: (b * tiles + t, 0)),
                  pl.BlockSpec((tile, 1), lambda b, t: (b * tiles + t, 0)),
                  const((1, MLA_Q_RANK)), const((1, MLA_KV_RANK)),
                  const((MLA_Q_RANK, MLA_WIDTH + MLA_HEADS * LANES)), const((MLA_KV_RANK, 2 * MLA_WIDTH)),
                  const((1, LANES)), const((1, LANES))],
        out_specs=[head_spec(MLA_QK_PAD), head_spec(MLA_NOPE),
                   pl.BlockSpec((1, tile, LANES), lambda b, t: (b, t, 0)), head_spec(MLA_V)],
        out_shape=[jax.ShapeDtypeStruct((batch, MLA_HEADS, seq, MLA_QK_PAD), BF16),
                   jax.ShapeDtypeStruct((batch, MLA_HEADS, seq, MLA_NOPE), BF16),
                   jax.ShapeDtypeStruct((batch, seq, LANES), BF16),
                   jax.ShapeDtypeStruct((batch, MLA_HEADS, seq, MLA_V), BF16)],
        compiler_params=_cparams(("parallel", "parallel")),
        name="mla_qkv",
    )(proj, kr, pos, q_norm_w, kv_norm_w, w_uq, w_ukv, freq, sign)


def _causal_attention_kernel(q_ref, k_ref, kr_ref, v_ref, o_ref, s_ref, m_ref, acc_ref):
    chains = 2
    tk = q_ref.shape[2] // chains
    qi = pl.program_id(2)
    m_ref[...] = jnp.full(m_ref.shape, -1e30, F32)
    acc_ref[...] = jnp.zeros(acc_ref.shape, F32)
    ones = jnp.ones((tk, MLA_V), BF16)

    def kv_rows(j):
        return pl.ds(pl.multiple_of(j * tk, tk), tk)

    def scores(c, j, buf):
        rows = kv_rows(j)
        k = jnp.concatenate([k_ref[0, 0, rows, :], kr_ref[0, rows, :]], axis=1)
        s_ref[buf, c] = _dot_nt(q_ref[0, 0, c * tk:(c + 1) * tk, :], k)

    def softmax_pv(c, j, buf, diagonal):
        s = s_ref[buf, c]
        if diagonal:
            row = lax.broadcasted_iota(jnp.int32, (tk, tk), 0)
            col = lax.broadcasted_iota(jnp.int32, (tk, tk), 1)
            s = jnp.where(col <= row, s, -1e30)
        m = m_ref[c]
        m_new = jnp.maximum(m, jnp.max(s, axis=-1, keepdims=True))
        m_ref[c] = m_new
        p = jnp.exp2(s - jnp.concatenate([m_new] * (tk // LANES), axis=1)).astype(BF16)
        alpha = jnp.exp2(m - m_new)
        pv = _dot(p, jnp.concatenate([v_ref[0, 0, kv_rows(j), :], ones], axis=1))
        for half in range(2):
            lanes = slice(half * MLA_V, (half + 1) * MLA_V)
            acc_ref[c, :, lanes] = alpha * acc_ref[c, :, lanes] + pv[:, lanes]

    for c in range(chains):
        scores(c, 0, 0)

    def pairs(count):
        def body(jj, carry):
            for u in range(count):
                j = 2 * (count * jj + u)
                for cur in range(2):
                    for c in range(chains):
                        scores(c, j + cur + 1, 1 - cur)
                    for c in range(chains):
                        softmax_pv(c, j + cur, cur, False)
            return carry
        return body

    lax.fori_loop(0, qi // 2, pairs(2), 0)
    lax.fori_loop(qi // 2 * 2, qi, pairs(1), 0)
    j = 2 * qi
    scores(1, j + 1, 1)
    softmax_pv(0, j, 0, True)
    softmax_pv(1, j, 0, False)
    softmax_pv(1, j + 1, 1, True)
    for c in range(chains):
        o_ref[c * tk:(c + 1) * tk, :] = (acc_ref[c, :, 0:MLA_V] / acc_ref[c, :, MLA_V:]).astype(o_ref.dtype)


def _causal_attention(q, k, k_rope, v, tq):
    batch, heads, seq, _ = q.shape
    nq = seq // tq
    tk = tq // 2
    return pl.pallas_call(
        _causal_attention_kernel,
        grid=(batch, heads, nq),
        in_specs=[pl.BlockSpec((1, 1, tq, MLA_QK_PAD), lambda b, h, i: (b, h, i, 0)),
                  pl.BlockSpec((1, 1, seq, MLA_NOPE), lambda b, h, i: (b, h, 0, 0)),
                  pl.BlockSpec((1, seq, LANES), lambda b, h, i: (b, 0, 0)),
                  pl.BlockSpec((1, 1, seq, MLA_V), lambda b, h, i: (b, h, 0, 0))],
        out_specs=pl.BlockSpec((tq, MLA_V), lambda b, h, i: (b * nq + i, h)),
        out_shape=jax.ShapeDtypeStruct((batch * seq, heads * MLA_V), BF16),
        scratch_shapes=[pltpu.VMEM((2, 2, tk, tk), F32),
                        pltpu.VMEM((2, tk, LANES), F32),
                        pltpu.VMEM((2, tk, 2 * MLA_V), F32)],
        compiler_params=_cparams(("parallel", "parallel", "arbitrary")),
        name="causal_attention",
    )(q, k, k_rope, v)


def _odd_tail_kernel(a_ref, p_ref, k_ref, v_ref, x_ref, w_ref, fw_ref, out_ref):
    y_c = a_ref[...].astype(F32) * _silu(p_ref[:, _O_ZC:_O_ZC + MLA_WIDTH].astype(F32))
    acc = x_ref[...] + _dot(y_c.astype(BF16), w_ref[0:MLA_WIDTH, :])
    y_m = _mem_attention(p_ref[:, _O_MQ1:_O_MQ1 + MEM_WIDTH].astype(F32),
                         p_ref[:, _O_MZ1:_O_MZ1 + MEM_WIDTH].astype(F32), k_ref[0], v_ref[0])
    y_m = jnp.concatenate(y_m, axis=1).astype(BF16)
    acc = acc + _dot(y_m, w_ref[MLA_WIDTH:ODD_OUT, :])
    out_ref[...] = _rms(acc, fw_ref[...])


def _odd_tail(attn, proj, mem_kv, x, w_out, final_w, batch, tile):
    n = x.shape[0]
    tiles = n // batch // tile
    const = lambda shape: pl.BlockSpec(shape, lambda b, t: (0,) * len(shape))
    return pl.pallas_call(
        _odd_tail_kernel,
        grid=(batch, tiles),
        in_specs=[pl.BlockSpec((tile, MLA_WIDTH), lambda b, t: (b * tiles + t, 0)),
                  pl.BlockSpec((tile, ODD_MAIN), lambda b, t: (b * tiles + t, 0)),
                  pl.BlockSpec((1, N_MEM, MEM_WIDTH), lambda b, t: (b, 0, 2)),
                  pl.BlockSpec((1, N_MEM, MEM_WIDTH), lambda b, t: (b, 0, 3)),
                  pl.BlockSpec((tile, D_MODEL), lambda b, t: (b * tiles + t, 0)),
                  _resident((ODD_OUT, D_MODEL)), const((1, D_MODEL))],
        out_specs=pl.BlockSpec((tile, D_MODEL), lambda b, t: (b * tiles + t, 0)),
        out_shape=jax.ShapeDtypeStruct((n, D_MODEL), F32),
        compiler_params=_cparams(("parallel", "parallel")),
        name="odd_tail",
    )(attn, proj, mem_kv, mem_kv, x, w_out, final_w)


def _pad_cols(w, width):
    return jnp.pad(w, ((0, 0), (0, width - w.shape[1])))


def _expansion_matrix(width_per_head):
    e = np.zeros((LANES, SSD_HEADS * width_per_head), np.float32)
    for h in range(SSD_HEADS):
        e[h, h * width_per_head:(h + 1) * width_per_head] = 1.0
    return jnp.asarray(e, BF16)


def kernel(x, mem, positions, mem_norm_w, norm0_w, w_in0, sc_conv_w, ssd_conv_w, ssd_conv_b, ssd_dt_bias,
           ssd_a_log, ssd_d, ssd_norm_w, mem_k0, mem_v0, w_out0, norm1_w, w_in1, mla_q_norm_w,
           mla_kv_norm_w, mla_w_uq, mla_w_ukv, mem_k1, mem_v1, w_out1, final_norm_w):
    batch, seq, d = x.shape
    n = batch * seq
    x2d = x.reshape(n, d)
    row = lambda v: v.reshape(1, -1).astype(F32)

    mem_kv = _mem_kv(mem, mem_norm_w, jnp.concatenate([mem_k0, mem_v0, mem_k1, mem_v1], axis=1).astype(BF16))

    o_dt = _O_MQ0
    w0_main = jnp.concatenate([w_in0[:, :o_dt], w_in0[:, o_dt + SSD_HEADS:]], axis=1).astype(BF16)
    w0_dt = _pad_cols(w_in0[:, o_dt:o_dt + SSD_HEADS], LANES).astype(BF16)
    proj0, dt_raw = _norm_matmul(x2d, norm0_w, w0_main, w0_dt, tm=min(1024, n), tn=2048)
    pad_heads = lambda v: _pad_cols(row(v), LANES)
    y0 = _even_mixer(proj0, dt_raw, mem_kv, sc_conv_w, ssd_conv_w, row(ssd_conv_b), pad_heads(ssd_dt_bias),
                     pad_heads(ssd_a_log), row(jnp.repeat(ssd_d, SSD_HEAD_DIM)), row(ssd_norm_w),
                     _expansion_matrix(SSD_HEAD_DIM), _expansion_matrix(SSD_CHUNK),
                     jnp.asarray(np.tril(np.ones((SSD_CHUNK, SSD_CHUNK), np.float32)), BF16),
                     batch, tile=min(256, seq))
    x1 = _residual_matmul(y0, w_out0.astype(BF16), x2d, tm=min(512, n))

    o_kr = MLA_Q_RANK + MLA_KV_RANK
    w1_main = jnp.concatenate([w_in1[:, :o_kr], w_in1[:, o_kr + MLA_ROPE:]], axis=1).astype(BF16)
    w1_kr = _pad_cols(w_in1[:, o_kr:o_kr + MLA_ROPE], LANES).astype(BF16)
    proj1, kr = _norm_matmul(x1, norm1_w, w1_main, w1_kr, tm=min(1024, n), tn=ODD_MAIN // 2)
    wq = mla_w_uq.reshape(MLA_Q_RANK, MLA_HEADS, MLA_NOPE + MLA_ROPE)
    wq_rope = jnp.pad(wq[:, :, MLA_NOPE:], ((0, 0), (0, 0), (0, LANES - MLA_ROPE)))
    wq = jnp.concatenate([wq[:, :, :MLA_NOPE].reshape(MLA_Q_RANK, MLA_WIDTH),
                          wq_rope.reshape(MLA_Q_RANK, MLA_HEADS * LANES)], axis=1).astype(BF16)
    wkv = mla_w_ukv.reshape(MLA_KV_RANK, MLA_HEADS, MLA_NOPE + MLA_V)
    wkv = jnp.concatenate([wkv[:, :, :MLA_NOPE].reshape(MLA_KV_RANK, MLA_WIDTH),
                           wkv[:, :, MLA_NOPE:].reshape(MLA_KV_RANK, MLA_WIDTH)], axis=1).astype(BF16)
    half = MLA_ROPE // 2
    inv = ROPE_THETA ** (-jnp.arange(half, dtype=F32) / half)
    freq = jnp.concatenate([inv, inv, jnp.zeros((LANES - MLA_ROPE,), F32)]).reshape(1, LANES)
    sign = np.zeros((1, LANES), np.float32)
    sign[0, :half] = -1.0
    sign[0, half:MLA_ROPE] = 1.0
    q, k, k_rope, v = _mla_qkv(proj1, kr, positions.reshape(n, 1), row(mla_q_norm_w), row(mla_kv_norm_w), wq, wkv,
                       freq, jnp.asarray(sign), batch, tile=min(512, seq))
    attn = _causal_attention(q, k, k_rope, v, tq=min(1024, seq))
    out = _odd_tail(attn, proj1, mem_kv, x1, w_out1.astype(BF16), row(final_norm_w), batch, tile=min(512, seq))
    return out.reshape(batch, seq, d)
```

```python
import functools

import numpy as np
import jax
import jax.numpy as jnp
from jax import lax
from jax.experimental import pallas as pl
from jax.experimental.pallas import tpu as pltpu

F32 = jnp.float32
BF16 = jnp.bfloat16

D_MODEL = 2048
N_MEM = 256
EPS = 1e-6
SC_WIDTH = 1024
SC_KERNEL = 3
SSD_HEADS = 32
SSD_HEAD_DIM = 64
SSD_WIDTH = SSD_HEADS * SSD_HEAD_DIM
SSD_STATE = 128
SSD_GROUPS = 4
SSD_CONV = 4
SSD_CHUNK = 128
SSD_CONV_DIM = SSD_WIDTH + 2 * SSD_GROUPS * SSD_STATE
SSD_GROUP_WIDTH = SSD_WIDTH // SSD_GROUPS
MLA_HEADS = 16
MLA_Q_RANK = 768
MLA_KV_RANK = 512
MLA_NOPE = 128
MLA_ROPE = 64
MLA_V = 128
MLA_WIDTH = MLA_HEADS * MLA_V
MLA_QK_PAD = 256
ROPE_THETA = 10000.0
MEM_HEADS = 4
MEM_HEAD_DIM = 128
MEM_WIDTH = MEM_HEADS * MEM_HEAD_DIM
EVEN_MAIN = 4 * SC_WIDTH + SSD_WIDTH + SSD_CONV_DIM + 2 * MEM_WIDTH
EVEN_OUT = SC_WIDTH + SSD_WIDTH + MEM_WIDTH
ODD_MAIN = MLA_Q_RANK + MLA_KV_RANK + MLA_WIDTH + 2 * MEM_WIDTH
ODD_OUT = MLA_WIDTH + MEM_WIDTH
LANES = 128
HALO = 8

VMEM_LIMIT = 56 * 1024 * 1024


def _cparams(sem):
    return pltpu.CompilerParams(dimension_semantics=sem, vmem_limit_bytes=VMEM_LIMIT)


def _dot(a, b):
    return jnp.dot(a, b, preferred_element_type=F32)


def _dot_nt(a, b):
    return lax.dot_general(a, b, (((1,), (1,)), ((), ())), preferred_element_type=F32)


def _rms(x, w):
    return x * lax.rsqrt(jnp.mean(x * x, axis=-1, keepdims=True) + EPS) * w


def _silu(z):
    h = 0.5 * z
    return h + h * jnp.tanh(h)


def _split_bf16(v, n):
    parts = []
    r = v
    for _ in range(n):
        p = r.astype(BF16)
        parts.append(p)
        r = r - p.astype(F32)
    return parts


def _expand(v, e_ref, n):
    e = e_ref[...]
    out = None
    for p in _split_bf16(v, n):
        t = _dot(p, e)
        out = t if out is None else out + t
    return out


def _mem_kv_kernel(mem_ref, nw_ref, w_ref, out_ref):
    m = _rms(mem_ref[0], nw_ref[...])
    out_ref[0] = _dot(m.astype(BF16), w_ref[...]).astype(BF16)


def _mem_kv(mem, mem_norm_w, w_kv):
    b, m, d = mem.shape
    n = w_kv.shape[1]
    tn = MEM_WIDTH
    return pl.pallas_call(
        _mem_kv_kernel,
        grid=(b, n // tn),
        in_specs=[pl.BlockSpec((1, m, d), lambda i, j: (i, 0, 0)),
                  pl.BlockSpec((1, d), lambda i, j: (0, 0)),
                  pl.BlockSpec((d, tn), lambda i, j: (0, j))],
        out_specs=pl.BlockSpec((1, m, tn), lambda i, j: (i, 0, j)),
        out_shape=jax.ShapeDtypeStruct((b, m, n), BF16),
        compiler_params=_cparams(("parallel", "arbitrary")),
        name="mem_kv",
    )(mem, mem_norm_w.reshape(1, d), w_kv)


def _norm_matmul_kernel(x_ref, nw_ref, w_ref, ws_ref, out_ref, side_ref, xn_ref):
    @pl.when(pl.program_id(1) == 0)
    def _():
        xn_ref[...] = _rms(x_ref[...], nw_ref[...]).astype(BF16)
        side_ref[...] = _dot(xn_ref[...], ws_ref[...])

    out_ref[...] = _dot(xn_ref[...], w_ref[...]).astype(out_ref.dtype)


def _norm_matmul(x, norm_w, w, w_side, tm, tn):
    n, d = x.shape
    c = w.shape[1]
    cs = w_side.shape[1]
    return pl.pallas_call(
        _norm_matmul_kernel,
        grid=(n // tm, c // tn),
        in_specs=[pl.BlockSpec((tm, d), lambda i, j: (i, 0)),
                  pl.BlockSpec((1, d), lambda i, j: (0, 0)),
                  pl.BlockSpec((d, tn), lambda i, j: (0, j)),
                  pl.BlockSpec((d, cs), lambda i, j: (0, 0))],
        out_specs=[pl.BlockSpec((tm, tn), lambda i, j: (i, j)),
                   pl.BlockSpec((tm, cs), lambda i, j: (i, 0))],
        out_shape=[jax.ShapeDtypeStruct((n, c), BF16),
                   jax.ShapeDtypeStruct((n, cs), F32)],
        scratch_shapes=[pltpu.VMEM((tm, d), BF16)],
        compiler_params=_cparams(("parallel", "arbitrary")),
        name="norm_matmul",
    )(x, norm_w.reshape(1, d), w, w_side)


def _mem_attention(mq, mz, k, v):
    outs = []
    for h in range(MEM_HEADS):
        sl = slice(h * MEM_HEAD_DIM, (h + 1) * MEM_HEAD_DIM)
        q = (mq[:, sl] * (MEM_HEAD_DIM ** -0.5)).astype(BF16)
        s = _dot_nt(q, k[:, sl])
        p = jnp.exp(s - jnp.max(s, axis=-1, keepdims=True))
        o = _dot(p.astype(BF16), v[:, sl]) * (1.0 / jnp.sum(p, axis=-1, keepdims=True))
        outs.append(o * _silu(mz[:, sl]))
    return outs


_O_SCB, _O_SCC, _O_SCV, _O_SCZ = 0, SC_WIDTH, 2 * SC_WIDTH, 3 * SC_WIDTH
_O_SSDZ = 4 * SC_WIDTH
_O_XBC = _O_SSDZ + SSD_WIDTH
_O_MQ0 = _O_XBC + SSD_CONV_DIM
_O_MZ0 = _O_MQ0 + MEM_WIDTH


def _even_mixer_kernel(p_ref, dt_ref, k_ref, v_ref, scw_ref, xw_ref, xb_ref, dtb_ref, alog_ref, d_ref,
                       nw_ref, ehp_ref, ehk_ref, tri_ref, shift_ref, y_ref, cv_buf, xbc_tail, state):
    tile = p_ref.shape[0]

    @pl.when(pl.program_id(1) == 0)
    def _():
        cv_buf[0:HALO, :] = jnp.zeros((HALO, SC_WIDTH), F32)
        xbc_tail[...] = jnp.zeros(xbc_tail.shape, BF16)
        state[...] = jnp.zeros(state.shape, F32)

    cv_buf[HALO:HALO + tile, :] = (p_ref[:, _O_SCC:_O_SCC + SC_WIDTH].astype(F32)
                                   * p_ref[:, _O_SCV:_O_SCV + SC_WIDTH].astype(F32))
    conv = None
    for k in range(SC_KERNEL):
        off = HALO - (SC_KERNEL - 1) + k
        t = scw_ref[k:k + 1, :] * cv_buf[off:off + tile, :]
        conv = t if conv is None else conv + t
    y_a = (p_ref[:, _O_SCB:_O_SCB + SC_WIDTH].astype(F32) * conv
           * _silu(p_ref[:, _O_SCZ:_O_SCZ + SC_WIDTH].astype(F32)))
    y_ref[:, 0:SC_WIDTH] = y_a.astype(BF16)
    cv_buf[0:HALO, :] = cv_buf[tile:tile + HALO, :]

    y_m = _mem_attention(p_ref[:, _O_MQ0:_O_MQ0 + MEM_WIDTH].astype(F32),
                         p_ref[:, _O_MZ0:_O_MZ0 + MEM_WIDTH].astype(F32), k_ref[0], v_ref[0])
    for h in range(MEM_HEADS):
        c0 = SC_WIDTH + SSD_WIDTH + h * MEM_HEAD_DIM
        y_ref[:, c0:c0 + MEM_HEAD_DIM] = y_m[h].astype(BF16)

    xbc_cols = slice(_O_XBC, _O_XBC + SSD_CONV_DIM)
    a_row = -jnp.exp(alog_ref[...])
    row = lax.broadcasted_iota(jnp.int32, (SSD_CHUNK, SSD_CHUNK), 0)
    col = lax.broadcasted_iota(jnp.int32, (SSD_CHUNK, SSD_CHUNK), 1)
    causal = row >= col
    first_head = col < SSD_HEAD_DIM
    heads_per_group = SSD_HEADS // SSD_GROUPS
    for c in range(tile // SSD_CHUNK):
        r0 = c * SSD_CHUNK
        cur = p_ref[r0:r0 + SSD_CHUNK, xbc_cols]
        prev = xbc_tail[...] if c == 0 else p_ref[r0 - SSD_CHUNK:r0, xbc_cols]
        shifted = _dot(shift_ref[...], jnp.concatenate([prev, cur], axis=0))
        u = xb_ref[...] + xw_ref[SSD_CONV - 1:SSD_CONV, :] * cur.astype(F32)
        for k in range(SSD_CONV - 1):
            u = u + xw_ref[k:k + 1, :] * shifted[k * SSD_CHUNK:(k + 1) * SSD_CHUNK, :]
        xbc = _silu(u)
        xs = xbc[:, 0:SSD_WIDTH]
        bm = xbc[:, SSD_WIDTH:SSD_WIDTH + SSD_GROUPS * SSD_STATE]
        cm = xbc[:, SSD_WIDTH + SSD_GROUPS * SSD_STATE:]
        dt_in = dt_ref[r0:r0 + SSD_CHUNK, :] + dtb_ref[...]
        dt = jnp.maximum(dt_in, 0.0) + jnp.log1p(jnp.exp(-jnp.abs(dt_in)))
        dta = dt * a_row
        tri = tri_ref[...]
        a_cum = None
        for piece in _split_bf16(dta, 3):
            t = _dot(tri, piece)
            a_cum = t if a_cum is None else a_cum + t
        a_cum_t = a_cum.T
        a_last = a_cum[SSD_CHUNK - 1:SSD_CHUNK, :]
        exp_a = jnp.exp(a_cum)
        dt_e = _expand(dt, ehp_ref, 2)
        exp_a_e = _expand(exp_a, ehp_ref, 2)
        dtd_e = _expand(dt * jnp.exp(a_last - a_cum), ehp_ref, 2)
        a_q = _expand(a_cum, ehk_ref, 3)
        xdt = xs * dt_e
        xdte = xs * dtd_e
        chunk_decay_e = exp_a_e[SSD_CHUNK - 1:SSD_CHUNK, :]
        z = p_ref[r0:r0 + SSD_CHUNK, _O_SSDZ:_O_SSDZ + SSD_WIDTH].astype(F32)
        for g in range(SSD_GROUPS):
            gs = slice(g * SSD_GROUP_WIDTH, (g + 1) * SSD_GROUP_WIDTH)
            bm_g = bm[:, g * SSD_STATE:(g + 1) * SSD_STATE]
            cm_g = cm[:, g * SSD_STATE:(g + 1) * SSD_STATE].astype(BF16)
            cb = _dot_nt(cm_g, bm_g.astype(BF16))
            s_enter = state[g]
            y_off = _dot(cm_g, s_enter.astype(BF16)) * exp_a_e[:, gs]
            state[g] = s_enter * chunk_decay_e[:, gs] + _dot(bm_g.T.astype(BF16), xdte[:, gs].astype(BF16))
            pairs = []
            for j in range(heads_per_group // 2):
                h0 = g * heads_per_group + 2 * j
                lmat = []
                for h in (h0, h0 + 1):
                    seg = a_q[:, h * SSD_CHUNK:(h + 1) * SSD_CHUNK] - a_cum_t[h:h + 1, :]
                    lmat.append((cb * jnp.exp(jnp.where(causal, seg, -jnp.inf))).astype(BF16))
                xp = xdt[:, h0 * SSD_HEAD_DIM:(h0 + 2) * SSD_HEAD_DIM]
                rhs = jnp.concatenate([jnp.where(first_head, xp, 0.0), jnp.where(first_head, 0.0, xp)],
                                      axis=0).astype(BF16)
                pairs.append(_dot(jnp.concatenate(lmat, axis=1), rhs))
            y_g = (jnp.concatenate(pairs, axis=1) + y_off + xs[:, gs] * d_ref[:, gs]) * _silu(z[:, gs])
            y_g = y_g * lax.rsqrt(jnp.mean(y_g * y_g, axis=-1, keepdims=True) + EPS) * nw_ref[:, gs]
            c0 = SC_WIDTH + g * SSD_GROUP_WIDTH
            y_ref[r0:r0 + SSD_CHUNK, c0:c0 + SSD_GROUP_WIDTH] = y_g.astype(BF16)
    xbc_tail[...] = p_ref[tile - SSD_CHUNK:tile, xbc_cols]


def _even_mixer(proj, dt_raw, mem_kv, sc_conv_w, ssd_conv_w, ssd_conv_b, dt_bias, a_log, d_e, ssd_norm_w,
                e_hp, e_hk, tri, shift, batch, tile):
    n = proj.shape[0]
    tiles = n // batch // tile
    const = lambda shape: pl.BlockSpec(shape, lambda b, t: (0,) * len(shape))
    return pl.pallas_call(
        _even_mixer_kernel,
        grid=(batch, tiles),
        in_specs=[pl.BlockSpec((tile, EVEN_MAIN), lambda b, t: (b * tiles + t, 0)),
                  pl.BlockSpec((tile, LANES), lambda b, t: (b * tiles + t, 0)),
                  pl.BlockSpec((1, N_MEM, MEM_WIDTH), lambda b, t: (b, 0, 0)),
                  pl.BlockSpec((1, N_MEM, MEM_WIDTH), lambda b, t: (b, 0, 1)),
                  const((SC_KERNEL, SC_WIDTH)), const((SSD_CONV, SSD_CONV_DIM)), const((1, SSD_CONV_DIM)),
                  const((1, LANES)), const((1, LANES)), const((1, SSD_WIDTH)), const((1, SSD_WIDTH)),
                  const((LANES, SSD_WIDTH)), const((LANES, SSD_HEADS * SSD_CHUNK)),
                  const((SSD_CHUNK, SSD_CHUNK)), const(((SSD_CONV - 1) * SSD_CHUNK, 2 * SSD_CHUNK))],
        out_specs=pl.BlockSpec((tile, EVEN_OUT), lambda b, t: (b * tiles + t, 0)),
        out_shape=jax.ShapeDtypeStruct((n, EVEN_OUT), BF16),
        scratch_shapes=[pltpu.VMEM((tile + HALO, SC_WIDTH), F32),
                        pltpu.VMEM((SSD_CHUNK, SSD_CONV_DIM), BF16),
                        pltpu.VMEM((SSD_GROUPS, SSD_STATE, SSD_GROUP_WIDTH), F32)],
        compiler_params=_cparams(("parallel", "arbitrary")),
        name="even_mixer",
    )(proj, dt_raw, mem_kv, mem_kv, sc_conv_w, ssd_conv_w, ssd_conv_b, dt_bias, a_log, d_e, ssd_norm_w,
      e_hp, e_hk, tri, shift)


def _residual_matmul_kernel(y_ref, w_ref, x_ref, out_ref):
    out_ref[...] = x_ref[...] + _dot(y_ref[...], w_ref[...])


def _resident(shape):
    return pl.BlockSpec(shape, lambda *_: (0,) * len(shape), pipeline_mode=pl.Buffered(1))


def _residual_matmul(y, w, x, tm):
    n, k = y.shape
    d = w.shape[1]
    return pl.pallas_call(
        _residual_matmul_kernel,
        grid=(n // tm,),
        in_specs=[pl.BlockSpec((tm, k), lambda i: (i, 0)),
                  _resident((k, d)),
                  pl.BlockSpec((tm, d), lambda i: (i, 0))],
        out_specs=pl.BlockSpec((tm, d), lambda i: (i, 0)),
        out_shape=jax.ShapeDtypeStruct((n, d), F32),
        compiler_params=_cparams(("parallel",)),
        name="residual_matmul",
    )(y, w, x)


_O_CQ, _O_CKV = 0, MLA_Q_RANK
_O_ZC = MLA_Q_RANK + MLA_KV_RANK
_O_MQ1 = _O_ZC + MLA_WIDTH
_O_MZ1 = _O_MQ1 + MEM_WIDTH


def _rope_lanes(a, cos_t, sin_t, low_half):
    half = MLA_ROPE // 2
    swapped = jnp.where(low_half, pltpu.roll(a, LANES - half, 1), pltpu.roll(a, half, 1))
    return a * cos_t + swapped * sin_t


def _mla_qkv_kernel(p_ref, kr_ref, pos_ref, qnw_ref, kvnw_ref, wq_ref, wkv_ref, freq_ref, sign_ref,
                    q_ref, k_ref, kro_ref, v_ref):
    tile = p_ref.shape[0]
    cq = _rms(p_ref[:, _O_CQ:_O_CQ + MLA_Q_RANK].astype(F32), qnw_ref[...])
    q = _dot(cq.astype(BF16), wq_ref[...])
    ckv = _rms(p_ref[:, _O_CKV:_O_CKV + MLA_KV_RANK].astype(F32), kvnw_ref[...])
    kv = _dot(ckv.astype(BF16), wkv_ref[...])
    ang = pos_ref[...].astype(F32) * freq_ref[...]
    cos_t = jnp.cos(ang)
    sin_t = jnp.sin(ang) * sign_ref[...]
    low_half = lax.broadcasted_iota(jnp.int32, (tile, LANES), 1) < MLA_ROPE // 2
    kro_ref[0] = _rope_lanes(kr_ref[...], cos_t, sin_t, low_half).astype(BF16)
    scale = (MLA_NOPE + MLA_ROPE) ** -0.5 * np.log2(np.e)
    for h in range(MLA_HEADS):
        nope = slice(h * MLA_NOPE, (h + 1) * MLA_NOPE)
        rope = slice(MLA_WIDTH + h * LANES, MLA_WIDTH + (h + 1) * LANES)
        q_ref[0, h, :, 0:MLA_NOPE] = (q[:, nope] * scale).astype(BF16)
        q_ref[0, h, :, MLA_NOPE:MLA_QK_PAD] = (_rope_lanes(q[:, rope], cos_t, sin_t, low_half) * scale).astype(BF16)
        k_ref[0, h] = kv[:, nope].astype(BF16)
        v_ref[0, h] = kv[:, MLA_WIDTH + h * MLA_V:MLA_WIDTH + (h + 1) * MLA_V].astype(BF16)


def _mla_qkv(proj, kr, pos, q_norm_w, kv_norm_w, w_uq, w_ukv, freq, sign, batch, tile):
    n = proj.shape[0]
    seq = n // batch
    tiles = seq // tile
    const = lambda shape: pl.BlockSpec(shape, lambda b, t: (0,) * len(shape))
    head_spec = lambda w: pl.BlockSpec((1, MLA_HEADS, tile, w), lambda b, t: (b, 0, t, 0))
    return pl.pallas_call(
        _mla_qkv_kernel,
        grid=(batch, tiles),
        in_specs=[pl.BlockSpec((tile, MLA_Q_RANK + MLA_KV_RANK), lambda b, t: (b * tiles + t, 0)),
                  pl.BlockSpec((tile, LANES), lambda b, t: (b * tiles + t, 0)),
                  pl.BlockSpec((tile, 1), lambda b, t: (b * tiles + t, 0)),
                  const((1, MLA_Q_RANK)), const((1, MLA_KV_RANK)),
                  const((MLA_Q_RANK, MLA_WIDTH + MLA_HEADS * LANES)), const((MLA_KV_RANK, 2 * MLA_WIDTH)),
                  const((1, LANES)), const((1, LANES))],
        out_specs=[head_spec(MLA_QK_PAD), head_spec(MLA_NOPE),
                   pl.BlockSpec((1, tile, LANES), lambda b, t: (b, t, 0)), head_spec(MLA_V)],
        out_shape=[jax.ShapeDtypeStruct((batch, MLA_HEADS, seq, MLA_QK_PAD), BF16),
                   jax.ShapeDtypeStruct((batch, MLA_HEADS, seq, MLA_NOPE), BF16),
                   jax.ShapeDtypeStruct((batch, seq, LANES), BF16),
                   jax.ShapeDtypeStruct((batch, MLA_HEADS, seq, MLA_V), BF16)],
        compiler_params=_cparams(("parallel", "parallel")),
        name="mla_qkv",
    )(proj, kr, pos, q_norm_w, kv_norm_w, w_uq, w_ukv, freq, sign)


def _causal_attention_kernel(q_ref, k_ref, kr_ref, v_ref, o_ref, s_ref, m_ref, acc_ref):
    chains = 2
    tk = o_ref.shape[0] // chains
    qi = pl.program_id(2)
    last = pl.num_programs(2) - 1
    m_ref[...] = jnp.full(m_ref.shape, -1e30, F32)
    acc_ref[...] = jnp.zeros(acc_ref.shape, F32)
    ones = jnp.ones((tk, MLA_V), BF16)

    def kv_rows(j):
        return pl.ds(pl.multiple_of(j * tk, tk), tk)

    def scores(c, j, buf, step=qi):
        rows = kv_rows(j)
        k = jnp.concatenate([k_ref[0, 0, rows, :], kr_ref[0, rows, :]], axis=1)
        s_ref[buf, c] = _dot_nt(q_ref[0, 0, kv_rows(step * chains + c), :], k)

    def softmax_pv(c, j, buf, diagonal):
        s = s_ref[buf, c]
        if diagonal:
            row = lax.broadcasted_iota(jnp.int32, (tk, tk), 0)
            col = lax.broadcasted_iota(jnp.int32, (tk, tk), 1)
            s = jnp.where(col <= row, s, -1e30)
        m = m_ref[c]
        m_new = jnp.maximum(m, jnp.max(s, axis=-1, keepdims=True))
        m_ref[c] = m_new
        p = jnp.exp2(s - jnp.concatenate([m_new] * (tk // LANES), axis=1)).astype(BF16)
        alpha = jnp.exp2(m - m_new)
        pv = _dot(p, jnp.concatenate([v_ref[0, 0, kv_rows(j), :], ones], axis=1))
        for half in range(2):
            lanes = slice(half * MLA_V, (half + 1) * MLA_V)
            acc_ref[c, :, lanes] = alpha * acc_ref[c, :, lanes] + pv[:, lanes]

    @pl.when(qi == 0)
    def _():
        for c in range(chains):
            scores(c, 0, 0)

    def pairs(count):
        def body(jj, carry):
            for u in range(count):
                j = 2 * (count * jj + u)
                for cur in range(2):
                    for c in range(chains):
                        scores(c, j + cur + 1, 1 - cur)
                    for c in range(chains):
                        softmax_pv(c, j + cur, cur, False)
            return carry
        return body

    lax.fori_loop(0, qi // 4, pairs(4), 0)
    lax.fori_loop(qi // 4 * 2, qi // 2, pairs(2), 0)
    lax.fori_loop(qi // 2 * 2, qi, pairs(1), 0)
    j = 2 * qi
    scores(1, j + 1, 1)
    softmax_pv(0, j, 0, True)
    softmax_pv(1, j, 0, False)
    softmax_pv(1, j + 1, 1, True)
    for c in range(chains):
        scores(c, 0, 0, step=jnp.minimum(qi + 1, last))
    for c in range(chains):
        o_ref[c * tk:(c + 1) * tk, :] = (acc_ref[c, :, 0:MLA_V] / acc_ref[c, :, MLA_V:]).astype(o_ref.dtype)


def _causal_attention(q, k, k_rope, v, tq):
    batch, heads, seq, _ = q.shape
    nq = seq // tq
    tk = tq // 2
    return pl.pallas_call(
        _causal_attention_kernel,
        grid=(batch, heads, nq),
        in_specs=[pl.BlockSpec((1, 1, seq, MLA_QK_PAD), lambda b, h, i: (b, h, 0, 0)),
                  pl.BlockSpec((1, 1, seq, MLA_NOPE), lambda b, h, i: (b, h, 0, 0)),
                  pl.BlockSpec((1, seq, LANES), lambda b, h, i: (b, 0, 0)),
                  pl.BlockSpec((1, 1, seq, MLA_V), lambda b, h, i: (b, h, 0, 0))],
        out_specs=pl.BlockSpec((tq, MLA_V), lambda b, h, i: (b * nq + i, h)),
        out_shape=jax.ShapeDtypeStruct((batch * seq, heads * MLA_V), BF16),
        scratch_shapes=[pltpu.VMEM((2, 2, tk, tk), F32),
                        pltpu.VMEM((2, tk, LANES), F32),
                        pltpu.VMEM((2, tk, 2 * MLA_V), F32)],
        compiler_params=_cparams(("parallel", "parallel", "arbitrary")),
        name="causal_attention",
    )(q, k, k_rope, v)


def _odd_tail_kernel(a_ref, p_ref, k_ref, v_ref, x_ref, w_ref, fw_ref, out_ref):
    y_c = a_ref[...].astype(F32) * _silu(p_ref[:, _O_ZC:_O_ZC + MLA_WIDTH].astype(F32))
    acc = x_ref[...] + _dot(y_c.astype(BF16), w_ref[0:MLA_WIDTH, :])
    y_m = _mem_attention(p_ref[:, _O_MQ1:_O_MQ1 + MEM_WIDTH].astype(F32),
                         p_ref[:, _O_MZ1:_O_MZ1 + MEM_WIDTH].astype(F32), k_ref[0], v_ref[0])
    y_m = jnp.concatenate(y_m, axis=1).astype(BF16)
    acc = acc + _dot(y_m, w_ref[MLA_WIDTH:ODD_OUT, :])
    out_ref[...] = _rms(acc, fw_ref[...])


def _odd_tail(attn, proj, mem_kv, x, w_out, final_w, batch, tile):
    n = x.shape[0]
    tiles = n // batch // tile
    const = lambda shape: pl.BlockSpec(shape, lambda b, t: (0,) * len(shape))
    return pl.pallas_call(
        _odd_tail_kernel,
        grid=(batch, tiles),
        in_specs=[pl.BlockSpec((tile, MLA_WIDTH), lambda b, t: (b * tiles + t, 0)),
                  pl.BlockSpec((tile, ODD_MAIN), lambda b, t: (b * tiles + t, 0)),
                  pl.BlockSpec((1, N_MEM, MEM_WIDTH), lambda b, t: (b, 0, 2)),
                  pl.BlockSpec((1, N_MEM, MEM_WIDTH), lambda b, t: (b, 0, 3)),
                  pl.BlockSpec((tile, D_MODEL), lambda b, t: (b * tiles + t, 0)),
                  _resident((ODD_OUT, D_MODEL)), const((1, D_MODEL))],
        out_specs=pl.BlockSpec((tile, D_MODEL), lambda b, t: (b * tiles + t, 0)),
        out_shape=jax.ShapeDtypeStruct((n, D_MODEL), F32),
        compiler_params=_cparams(("parallel", "parallel")),
        name="odd_tail",
    )(attn, proj, mem_kv, mem_kv, x, w_out, final_w)


def _pad_cols(w, width):
    return jnp.pad(w, ((0, 0), (0, width - w.shape[1])))


def _expansion_matrix(width_per_head):
    e = np.zeros((LANES, SSD_HEADS * width_per_head), np.float32)
    for h in range(SSD_HEADS):
        e[h, h * width_per_head:(h + 1) * width_per_head] = 1.0
    return jnp.asarray(e, BF16)


def _shift_matrix():
    s = np.zeros(((SSD_CONV - 1) * SSD_CHUNK, 2 * SSD_CHUNK), np.float32)
    for k in range(SSD_CONV - 1):
        for t in range(SSD_CHUNK):
            s[k * SSD_CHUNK + t, SSD_CHUNK + t - (SSD_CONV - 1 - k)] = 1.0
    return jnp.asarray(s, BF16)


def kernel(x, mem, positions, mem_norm_w, norm0_w, w_in0, sc_conv_w, ssd_conv_w, ssd_conv_b, ssd_dt_bias,
           ssd_a_log, ssd_d, ssd_norm_w, mem_k0, mem_v0, w_out0, norm1_w, w_in1, mla_q_norm_w,
           mla_kv_norm_w, mla_w_uq, mla_w_ukv, mem_k1, mem_v1, w_out1, final_norm_w):
    batch, seq, d = x.shape
    n = batch * seq
    x2d = x.reshape(n, d)
    row = lambda v: v.reshape(1, -1).astype(F32)

    mem_kv = _mem_kv(mem, mem_norm_w, jnp.concatenate([mem_k0, mem_v0, mem_k1, mem_v1], axis=1).astype(BF16))

    o_dt = _O_MQ0
    w_in0 = w_in0.astype(BF16)
    w0_main = jnp.concatenate([w_in0[:, :o_dt], w_in0[:, o_dt + SSD_HEADS:]], axis=1)
    w0_dt = _pad_cols(w_in0[:, o_dt:o_dt + SSD_HEADS], LANES)
    proj0, dt_raw = _norm_matmul(x2d, norm0_w, w0_main, w0_dt, tm=min(1024, n), tn=2048)
    pad_heads = lambda v: _pad_cols(row(v), LANES)
    y0 = _even_mixer(proj0, dt_raw, mem_kv, sc_conv_w, ssd_conv_w, row(ssd_conv_b), pad_heads(ssd_dt_bias),
                     pad_heads(ssd_a_log), row(jnp.repeat(ssd_d, SSD_HEAD_DIM)), row(ssd_norm_w),
                     _expansion_matrix(SSD_HEAD_DIM), _expansion_matrix(SSD_CHUNK),
                     jnp.asarray(np.tril(np.ones((SSD_CHUNK, SSD_CHUNK), np.float32)), BF16),
                     _shift_matrix(),
                     batch, tile=min(256, seq))
    x1 = _residual_matmul(y0, w_out0.astype(BF16), x2d, tm=min(512, n))

    o_kr = MLA_Q_RANK + MLA_KV_RANK
    w_in1 = w_in1.astype(BF16)
    w1_main = jnp.concatenate([w_in1[:, :o_kr], w_in1[:, o_kr + MLA_ROPE:]], axis=1)
    w1_kr = _pad_cols(w_in1[:, o_kr:o_kr + MLA_ROPE], LANES)
    proj1, kr = _norm_matmul(x1, norm1_w, w1_main, w1_kr, tm=min(1024, n), tn=ODD_MAIN // 2)
    wq = mla_w_uq.astype(BF16).reshape(MLA_Q_RANK, MLA_HEADS, MLA_NOPE + MLA_ROPE)
    wq_rope = jnp.pad(wq[:, :, MLA_NOPE:], ((0, 0), (0, 0), (0, LANES - MLA_ROPE)))
    wq = jnp.concatenate([wq[:, :, :MLA_NOPE].reshape(MLA_Q_RANK, MLA_WIDTH),
                          wq_rope.reshape(MLA_Q_RANK, MLA_HEADS * LANES)], axis=1)
    wkv = mla_w_ukv.astype(BF16).reshape(MLA_KV_RANK, MLA_HEADS, MLA_NOPE + MLA_V)
    wkv = jnp.concatenate([wkv[:, :, :MLA_NOPE].reshape(MLA_KV_RANK, MLA_WIDTH),
                           wkv[:, :, MLA_NOPE:].reshape(MLA_KV_RANK, MLA_WIDTH)], axis=1)
    half = MLA_ROPE // 2
    inv = ROPE_THETA ** (-jnp.arange(half, dtype=F32) / half)
    freq = jnp.concatenate([inv, inv, jnp.zeros((LANES - MLA_ROPE,), F32)]).reshape(1, LANES)
    sign = np.zeros((1, LANES), np.float32)
    sign[0, :half] = -1.0
    sign[0, half:MLA_ROPE] = 1.0
    q, k, k_rope, v = _mla_qkv(proj1, kr, positions.reshape(n, 1), row(mla_q_norm_w), row(mla_kv_norm_w), wq, wkv,
                       freq, jnp.asarray(sign), batch, tile=min(512, seq))
    attn = _causal_attention(q, k, k_rope, v, tq=min(1024, seq))
    out = _odd_tail(attn, proj1, mem_kv, x1, w_out1.astype(BF16), row(final_norm_w), batch, tile=min(512, seq))
    return out.reshape(batch, seq, d)
```

```python
import functools

import numpy as np
import jax
import jax.numpy as jnp
from jax import lax
from jax.experimental import pallas as pl
from jax.experimental.pallas import tpu as pltpu

F32 = jnp.float32
BF16 = jnp.bfloat16

D_MODEL = 2048
N_MEM = 256
EPS = 1e-6
SC_WIDTH = 1024
SC_KERNEL = 3
SSD_HEADS = 32
SSD_HEAD_DIM = 64
SSD_WIDTH = SSD_HEADS * SSD_HEAD_DIM
SSD_STATE = 128
SSD_GROUPS = 4
SSD_CONV = 4
SSD_CHUNK = 128
SSD_CONV_DIM = SSD_WIDTH + 2 * SSD_GROUPS * SSD_STATE
SSD_GROUP_WIDTH = SSD_WIDTH // SSD_GROUPS
MLA_HEADS = 16
MLA_Q_RANK = 768
MLA_KV_RANK = 512
MLA_NOPE = 128
MLA_ROPE = 64
MLA_V = 128
MLA_WIDTH = MLA_HEADS * MLA_V
MLA_QK_PAD = 256
ROPE_THETA = 10000.0
MEM_HEADS = 4
MEM_HEAD_DIM = 128
MEM_WIDTH = MEM_HEADS * MEM_HEAD_DIM
EVEN_MAIN = 4 * SC_WIDTH + SSD_WIDTH + SSD_CONV_DIM + 2 * MEM_WIDTH
EVEN_OUT = SC_WIDTH + SSD_WIDTH + MEM_WIDTH
ODD_MAIN = MLA_Q_RANK + MLA_KV_RANK + MLA_WIDTH + 2 * MEM_WIDTH
ODD_OUT = MLA_WIDTH + MEM_WIDTH
LANES = 128
HALO = 8

VMEM_LIMIT = 56 * 1024 * 1024


def _cparams(sem):
    return pltpu.CompilerParams(dimension_semantics=sem, vmem_limit_bytes=VMEM_LIMIT)


def _dot(a, b):
    return jnp.dot(a, b, preferred_element_type=F32)


def _dot_nt(a, b):
    return lax.dot_general(a, b, (((1,), (1,)), ((), ())), preferred_element_type=F32)


def _rms(x, w):
    return x * lax.rsqrt(jnp.mean(x * x, axis=-1, keepdims=True) + EPS) * w


def _silu(z):
    h = 0.5 * z
    return h + h * jnp.tanh(h)


def _split_bf16(v, n):
    parts = []
    r = v
    for _ in range(n):
        p = r.astype(BF16)
        parts.append(p)
        r = r - p.astype(F32)
    return parts


def _expand(v, e_ref, n):
    e = e_ref[...]
    out = None
    for p in _split_bf16(v, n):
        t = _dot(p, e)
        out = t if out is None else out + t
    return out


def _mem_kv_kernel(mem_ref, nw_ref, w_ref, out_ref):
    m = _rms(mem_ref[0], nw_ref[...])
    out_ref[0] = _dot(m.astype(BF16), w_ref[...]).astype(BF16)


def _mem_kv(mem, mem_norm_w, w_kv):
    b, m, d = mem.shape
    n = w_kv.shape[1]
    tn = MEM_WIDTH
    return pl.pallas_call(
        _mem_kv_kernel,
        grid=(b, n // tn),
        in_specs=[pl.BlockSpec((1, m, d), lambda i, j: (i, 0, 0)),
                  pl.BlockSpec((1, d), lambda i, j: (0, 0)),
                  pl.BlockSpec((d, tn), lambda i, j: (0, j))],
        out_specs=pl.BlockSpec((1, m, tn), lambda i, j: (i, 0, j)),
        out_shape=jax.ShapeDtypeStruct((b, m, n), BF16),
        compiler_params=_cparams(("parallel", "arbitrary")),
        name="mem_kv",
    )(mem, mem_norm_w.reshape(1, d), w_kv)


def _norm_matmul_kernel(x_ref, nw_ref, w_ref, ws_ref, out_ref, side_ref, xn_ref):
    @pl.when(pl.program_id(1) == 0)
    def _():
        xn_ref[...] = _rms(x_ref[...], nw_ref[...]).astype(BF16)
        side_ref[...] = _dot(xn_ref[...], ws_ref[...])

    out_ref[...] = _dot(xn_ref[...], w_ref[...]).astype(out_ref.dtype)


def _norm_matmul(x, norm_w, w, w_side, tm, tn):
    n, d = x.shape
    c = w.shape[1]
    cs = w_side.shape[1]
    return pl.pallas_call(
        _norm_matmul_kernel,
        grid=(n // tm, c // tn),
        in_specs=[pl.BlockSpec((tm, d), lambda i, j: (i, 0)),
                  pl.BlockSpec((1, d), lambda i, j: (0, 0)),
                  pl.BlockSpec((d, tn), lambda i, j: (0, j)),
                  pl.BlockSpec((d, cs), lambda i, j: (0, 0))],
        out_specs=[pl.BlockSpec((tm, tn), lambda i, j: (i, j)),
                   pl.BlockSpec((tm, cs), lambda i, j: (i, 0))],
        out_shape=[jax.ShapeDtypeStruct((n, c), BF16),
                   jax.ShapeDtypeStruct((n, cs), F32)],
        scratch_shapes=[pltpu.VMEM((tm, d), BF16)],
        compiler_params=_cparams(("parallel", "arbitrary")),
        name="norm_matmul",
    )(x, norm_w.reshape(1, d), w, w_side)


def _mem_attention(mq, mz, k, v):
    outs = []
    for h in range(MEM_HEADS):
        sl = slice(h * MEM_HEAD_DIM, (h + 1) * MEM_HEAD_DIM)
        q = (mq[:, sl] * (MEM_HEAD_DIM ** -0.5)).astype(BF16)
        s = _dot_nt(q, k[:, sl])
        p = jnp.exp(s - jnp.max(s, axis=-1, keepdims=True))
        o = _dot(p.astype(BF16), v[:, sl]) * (1.0 / jnp.sum(p, axis=-1, keepdims=True))
        outs.append(o * _silu(mz[:, sl]))
    return outs


_O_SCB, _O_SCC, _O_SCV, _O_SCZ = 0, SC_WIDTH, 2 * SC_WIDTH, 3 * SC_WIDTH
_O_SSDZ = 4 * SC_WIDTH
_O_XBC = _O_SSDZ + SSD_WIDTH
_O_MQ0 = _O_XBC + SSD_CONV_DIM
_O_MZ0 = _O_MQ0 + MEM_WIDTH


def _even_mixer_kernel(p_ref, dt_ref, k_ref, v_ref, *refs):
    consts = refs[:-4]
    y_ref, cv_buf, xbc_tail, state = refs[-4:]

    @pl.when(pl.program_id(1) == 0)
    def _():
        cv_buf[:, 0:HALO, :] = jnp.zeros((cv_buf.shape[0], HALO, SC_WIDTH), F32)
        xbc_tail[...] = jnp.zeros(xbc_tail.shape, BF16)
        state[...] = jnp.zeros(state.shape, F32)

    live = [_even_mixer_tile(p_ref.at[i], dt_ref.at[i], k_ref.at[i], v_ref.at[i], *consts,
                             y_ref.at[i], cv_buf.at[i], xbc_tail.at[i], state.at[i])
            for i in range(p_ref.shape[0])]
    while live:
        live = [g for g in live if next(g, "done") != "done"]


def _even_mixer_tile(p_ref, dt_ref, k_ref, v_ref, scw_ref, xw_ref, xb_ref, dtb_ref, alog_ref, d_ref,
                     nw_ref, ehp_ref, ehk_ref, tri_ref, shift_ref, y_ref, cv_buf, xbc_tail, state):
    tile = p_ref.shape[0]

    cv_buf[HALO:HALO + tile, :] = (p_ref[:, _O_SCC:_O_SCC + SC_WIDTH].astype(F32)
                                   * p_ref[:, _O_SCV:_O_SCV + SC_WIDTH].astype(F32))
    conv = None
    for k in range(SC_KERNEL):
        off = HALO - (SC_KERNEL - 1) + k
        t = scw_ref[k:k + 1, :] * cv_buf[off:off + tile, :]
        conv = t if conv is None else conv + t
    y_a = (p_ref[:, _O_SCB:_O_SCB + SC_WIDTH].astype(F32) * conv
           * _silu(p_ref[:, _O_SCZ:_O_SCZ + SC_WIDTH].astype(F32)))
    y_ref[:, 0:SC_WIDTH] = y_a.astype(BF16)
    cv_buf[0:HALO, :] = cv_buf[tile:tile + HALO, :]
    yield

    y_m = _mem_attention(p_ref[:, _O_MQ0:_O_MQ0 + MEM_WIDTH].astype(F32),
                         p_ref[:, _O_MZ0:_O_MZ0 + MEM_WIDTH].astype(F32), k_ref[...], v_ref[...])
    for h in range(MEM_HEADS):
        c0 = SC_WIDTH + SSD_WIDTH + h * MEM_HEAD_DIM
        y_ref[:, c0:c0 + MEM_HEAD_DIM] = y_m[h].astype(BF16)
    yield

    xbc_cols = slice(_O_XBC, _O_XBC + SSD_CONV_DIM)
    a_row = -jnp.exp(alog_ref[...])
    row = lax.broadcasted_iota(jnp.int32, (SSD_CHUNK, SSD_CHUNK), 0)
    col = lax.broadcasted_iota(jnp.int32, (SSD_CHUNK, SSD_CHUNK), 1)
    causal = row >= col
    first_head = col < SSD_HEAD_DIM
    heads_per_group = SSD_HEADS // SSD_GROUPS
    for c in range(tile // SSD_CHUNK):
        r0 = c * SSD_CHUNK
        cur = p_ref[r0:r0 + SSD_CHUNK, xbc_cols]
        prev = xbc_tail[...] if c == 0 else p_ref[r0 - SSD_CHUNK:r0, xbc_cols]
        shifted = _dot(shift_ref[...], jnp.concatenate([prev, cur], axis=0))
        u = xb_ref[...] + xw_ref[SSD_CONV - 1:SSD_CONV, :] * cur.astype(F32)
        for k in range(SSD_CONV - 1):
            u = u + xw_ref[k:k + 1, :] * shifted[k * SSD_CHUNK:(k + 1) * SSD_CHUNK, :]
        xbc = _silu(u)
        yield
        xs = xbc[:, 0:SSD_WIDTH]
        bm = xbc[:, SSD_WIDTH:SSD_WIDTH + SSD_GROUPS * SSD_STATE]
        cm = xbc[:, SSD_WIDTH + SSD_GROUPS * SSD_STATE:]
        dt_in = dt_ref[r0:r0 + SSD_CHUNK, :] + dtb_ref[...]
        dt = jnp.maximum(dt_in, 0.0) + jnp.log1p(jnp.exp(-jnp.abs(dt_in)))
        dta = dt * a_row
        tri = tri_ref[...]
        a_cum = None
        for piece in _split_bf16(dta, 3):
            t = _dot(tri, piece)
            a_cum = t if a_cum is None else a_cum + t
        a_cum_t = a_cum.T
        a_last = a_cum[SSD_CHUNK - 1:SSD_CHUNK, :]
        exp_a = jnp.exp(a_cum)
        dt_e = _expand(dt, ehp_ref, 2)
        exp_a_e = _expand(exp_a, ehp_ref, 2)
        dtd_e = _expand(dt * jnp.exp(a_last - a_cum), ehp_ref, 2)
        yield
        a_q = _expand(a_cum, ehk_ref, 3)
        yield
        xdt = xs * dt_e
        xdte = xs * dtd_e
        chunk_decay_e = exp_a_e[SSD_CHUNK - 1:SSD_CHUNK, :]
        z = p_ref[r0:r0 + SSD_CHUNK, _O_SSDZ:_O_SSDZ + SSD_WIDTH].astype(F32)
        for g in range(SSD_GROUPS):
            gs = slice(g * SSD_GROUP_WIDTH, (g + 1) * SSD_GROUP_WIDTH)
            bm_g = bm[:, g * SSD_STATE:(g + 1) * SSD_STATE]
            cm_g = cm[:, g * SSD_STATE:(g + 1) * SSD_STATE].astype(BF16)
            cb = _dot_nt(cm_g, bm_g.astype(BF16))
            s_enter = state[g]
            y_off = _dot(cm_g, s_enter.astype(BF16)) * exp_a_e[:, gs]
            state[g] = s_enter * chunk_decay_e[:, gs] + _dot(bm_g.T.astype(BF16), xdte[:, gs].astype(BF16))
            yield
            pairs = []
            for j in range(heads_per_group // 2):
                h0 = g * heads_per_group + 2 * j
                lmat = []
                for h in (h0, h0 + 1):
                    seg = a_q[:, h * SSD_CHUNK:(h + 1) * SSD_CHUNK] - a_cum_t[h:h + 1, :]
                    lmat.append((cb * jnp.exp(jnp.where(causal, seg, -jnp.inf))).astype(BF16))
                xp = xdt[:, h0 * SSD_HEAD_DIM:(h0 + 2) * SSD_HEAD_DIM]
                rhs = jnp.concatenate([jnp.where(first_head, xp, 0.0), jnp.where(first_head, 0.0, xp)],
                                      axis=0).astype(BF16)
                pairs.append(_dot(jnp.concatenate(lmat, axis=1), rhs))
                yield
            y_g = (jnp.concatenate(pairs, axis=1) + y_off + xs[:, gs] * d_ref[:, gs]) * _silu(z[:, gs])
            y_g = y_g * lax.rsqrt(jnp.mean(y_g * y_g, axis=-1, keepdims=True) + EPS) * nw_ref[:, gs]
            c0 = SC_WIDTH + g * SSD_GROUP_WIDTH
            y_ref[r0:r0 + SSD_CHUNK, c0:c0 + SSD_GROUP_WIDTH] = y_g.astype(BF16)
            yield
    xbc_tail[...] = p_ref[tile - SSD_CHUNK:tile, xbc_cols]


def _even_mixer(proj, dt_raw, mem_kv, sc_conv_w, ssd_conv_w, ssd_conv_b, dt_bias, a_log, d_e, ssd_norm_w,
                e_hp, e_hk, tri, shift, batch, tile):
    n = proj.shape[0]
    seq = n // batch
    rows = 2 if batch % 2 == 0 else 1
    const = lambda shape: pl.BlockSpec(shape, lambda b, t: (0,) * len(shape))
    tokens = lambda width: pl.BlockSpec((rows, tile, width), lambda b, t: (b, t, 0))
    y = pl.pallas_call(
        _even_mixer_kernel,
        grid=(batch // rows, seq // tile),
        in_specs=[tokens(EVEN_MAIN), tokens(LANES),
                  pl.BlockSpec((rows, N_MEM, MEM_WIDTH), lambda b, t: (b, 0, 0)),
                  pl.BlockSpec((rows, N_MEM, MEM_WIDTH), lambda b, t: (b, 0, 1)),
                  const((SC_KERNEL, SC_WIDTH)), const((SSD_CONV, SSD_CONV_DIM)), const((1, SSD_CONV_DIM)),
                  const((1, LANES)), const((1, LANES)), const((1, SSD_WIDTH)), const((1, SSD_WIDTH)),
                  const((LANES, SSD_WIDTH)), const((LANES, SSD_HEADS * SSD_CHUNK)),
                  const((SSD_CHUNK, SSD_CHUNK)), const(((SSD_CONV - 1) * SSD_CHUNK, 2 * SSD_CHUNK))],
        out_specs=tokens(EVEN_OUT),
        out_shape=jax.ShapeDtypeStruct((batch, seq, EVEN_OUT), BF16),
        scratch_shapes=[pltpu.VMEM((rows, tile + HALO, SC_WIDTH), F32),
                        pltpu.VMEM((rows, SSD_CHUNK, SSD_CONV_DIM), BF16),
                        pltpu.VMEM((rows, SSD_GROUPS, SSD_STATE, SSD_GROUP_WIDTH), F32)],
        compiler_params=_cparams(("parallel", "arbitrary")),
        name="even_mixer",
    )(proj.reshape(batch, seq, EVEN_MAIN), dt_raw.reshape(batch, seq, LANES), mem_kv, mem_kv, sc_conv_w,
      ssd_conv_w, ssd_conv_b, dt_bias, a_log, d_e, ssd_norm_w, e_hp, e_hk, tri, shift)
    return y.reshape(n, EVEN_OUT)


def _residual_matmul_kernel(y_ref, w_ref, x_ref, out_ref):
    out_ref[...] = x_ref[...] + _dot(y_ref[...], w_ref[...])


def _resident(shape):
    return pl.BlockSpec(shape, lambda *_: (0,) * len(shape), pipeline_mode=pl.Buffered(1))


def _residual_matmul(y, w, x, tm):
    n, k = y.shape
    d = w.shape[1]
    return pl.pallas_call(
        _residual_matmul_kernel,
        grid=(n // tm,),
        in_specs=[pl.BlockSpec((tm, k), lambda i: (i, 0)),
                  _resident((k, d)),
                  pl.BlockSpec((tm, d), lambda i: (i, 0))],
        out_specs=pl.BlockSpec((tm, d), lambda i: (i, 0)),
        out_shape=jax.ShapeDtypeStruct((n, d), F32),
        compiler_params=_cparams(("parallel",)),
        name="residual_matmul",
    )(y, w, x)


_O_CQ, _O_CKV = 0, MLA_Q_RANK
_O_ZC = MLA_Q_RANK + MLA_KV_RANK
_O_MQ1 = _O_ZC + MLA_WIDTH
_O_MZ1 = _O_MQ1 + MEM_WIDTH


def _rope_lanes(a, cos_t, sin_t, low_half):
    half = MLA_ROPE // 2
    swapped = jnp.where(low_half, pltpu.roll(a, LANES - half, 1), pltpu.roll(a, half, 1))
    return a * cos_t + swapped * sin_t


def _mla_qkv_kernel(p_ref, kr_ref, pos_ref, qnw_ref, kvnw_ref, wq_ref, wkv_ref, freq_ref, sign_ref,
                    q_ref, k_ref, kro_ref, v_ref):
    tile = p_ref.shape[0]
    cq = _rms(p_ref[:, _O_CQ:_O_CQ + MLA_Q_RANK].astype(F32), qnw_ref[...])
    q = _dot(cq.astype(BF16), wq_ref[...])
    ckv = _rms(p_ref[:, _O_CKV:_O_CKV + MLA_KV_RANK].astype(F32), kvnw_ref[...])
    kv = _dot(ckv.astype(BF16), wkv_ref[...])
    ang = pos_ref[...].astype(F32) * freq_ref[...]
    cos_t = jnp.cos(ang)
    sin_t = jnp.sin(ang) * sign_ref[...]
    low_half = lax.broadcasted_iota(jnp.int32, (tile, LANES), 1) < MLA_ROPE // 2
    kro_ref[0] = _rope_lanes(kr_ref[...], cos_t, sin_t, low_half).astype(BF16)
    scale = (MLA_NOPE + MLA_ROPE) ** -0.5 * np.log2(np.e)
    for h in range(MLA_HEADS):
        nope = slice(h * MLA_NOPE, (h + 1) * MLA_NOPE)
        rope = slice(MLA_WIDTH + h * LANES, MLA_WIDTH + (h + 1) * LANES)
        q_ref[0, h, :, 0:MLA_NOPE] = (q[:, nope] * scale).astype(BF16)
        q_ref[0, h, :, MLA_NOPE:MLA_QK_PAD] = (_rope_lanes(q[:, rope], cos_t, sin_t, low_half) * scale).astype(BF16)
        k_ref[0, h] = kv[:, nope].astype(BF16)
        v_ref[0, h] = kv[:, MLA_WIDTH + h * MLA_V:MLA_WIDTH + (h + 1) * MLA_V].astype(BF16)


def _mla_qkv(proj, kr, pos, q_norm_w, kv_norm_w, w_uq, w_ukv, freq, sign, batch, tile):
    n = proj.shape[0]
    seq = n // batch
    tiles = seq // tile
    const = lambda shape: pl.BlockSpec(shape, lambda b, t: (0,) * len(shape))
    head_spec = lambda w: pl.BlockSpec((1, MLA_HEADS, tile, w), lambda b, t: (b, 0, t, 0))
    return pl.pallas_call(
        _mla_qkv_kernel,
        grid=(batch, tiles),
        in_specs=[pl.BlockSpec((tile, MLA_Q_RANK + MLA_KV_RANK), lambda b, t: (b * tiles + t, 0)),
                  pl.BlockSpec((tile, LANES), lambda b, t: (b * tiles + t, 0)),
                  pl.BlockSpec((tile, 1), lambda b, t: (b * tiles + t, 0)),
                  const((1, MLA_Q_RANK)), const((1, MLA_KV_RANK)),
                  const((MLA_Q_RANK, MLA_WIDTH + MLA_HEADS * LANES)), const((MLA_KV_RANK, 2 * MLA_WIDTH)),
                  const((1, LANES)), const((1, LANES))],
        out_specs=[head_spec(MLA_QK_PAD), head_spec(MLA_NOPE),
                   pl.BlockSpec((1, tile, LANES), lambda b, t: (b, t, 0)), head_spec(MLA_V)],
        out_shape=[jax.ShapeDtypeStruct((batch, MLA_HEADS, seq, MLA_QK_PAD), BF16),
                   jax.ShapeDtypeStruct((batch, MLA_HEADS, seq, MLA_NOPE), BF16),
                   jax.ShapeDtypeStruct((batch, seq, LANES), BF16),
                   jax.ShapeDtypeStruct((batch, MLA_HEADS, seq, MLA_V), BF16)],
        compiler_params=_cparams(("parallel", "parallel")),
        name="mla_qkv",
    )(proj, kr, pos, q_norm_w, kv_norm_w, w_uq, w_ukv, freq, sign)


def _causal_attention_kernel(q_ref, k_ref, kr_ref, v_ref, o_ref, s_ref, m_ref, acc_ref):
    chains = 2
    tk = o_ref.shape[0] // chains
    qi = pl.program_id(2)
    last = pl.num_programs(2) - 1
    m_ref[...] = jnp.full(m_ref.shape, -1e30, F32)
    acc_ref[...] = jnp.zeros(acc_ref.shape, F32)
    ones = jnp.ones((tk, MLA_V), BF16)

    def kv_rows(j):
        return pl.ds(pl.multiple_of(j * tk, tk), tk)

    def scores(c, j, buf, step=qi):
        rows = kv_rows(j)
        k = jnp.concatenate([k_ref[0, 0, rows, :], kr_ref[0, rows, :]], axis=1)
        s_ref[buf, c] = _dot_nt(q_ref[0, 0, kv_rows(step * chains + c), :], k)

    def softmax_pv(c, j, buf, diagonal):
        s = s_ref[buf, c]
        if diagonal:
            row = lax.broadcasted_iota(jnp.int32, (tk, tk), 0)
            col = lax.broadcasted_iota(jnp.int32, (tk, tk), 1)
            s = jnp.where(col <= row, s, -1e30)
        m = m_ref[c]
        m_new = jnp.maximum(m, jnp.max(s, axis=-1, keepdims=True))
        m_ref[c] = m_new
        p = jnp.exp2(s - jnp.concatenate([m_new] * (tk // LANES), axis=1)).astype(BF16)
        alpha = jnp.exp2(m - m_new)
        pv = _dot(p, jnp.concatenate([v_ref[0, 0, kv_rows(j), :], ones], axis=1))
        for half in range(2):
            lanes = slice(half * MLA_V, (half + 1) * MLA_V)
            acc_ref[c, :, lanes] = alpha * acc_ref[c, :, lanes] + pv[:, lanes]

    @pl.when(qi == 0)
    def _():
        for c in range(chains):
            scores(c, 0, 0)

    def pairs(count):
        def body(jj, carry):
            for u in range(count):
                j = 2 * (count * jj + u)
                for cur in range(2):
                    for c in range(chains):
                        scores(c, j + cur + 1, 1 - cur)
                    for c in range(chains):
                        softmax_pv(c, j + cur, cur, False)
            return carry
        return body

    lax.fori_loop(0, qi // 4, pairs(4), 0)
    lax.fori_loop(qi // 4 * 2, qi // 2, pairs(2), 0)
    lax.fori_loop(qi // 2 * 2, qi, pairs(1), 0)
    j = 2 * qi
    scores(1, j + 1, 1)
    softmax_pv(0, j, 0, True)
    softmax_pv(1, j, 0, False)
    softmax_pv(1, j + 1, 1, True)
    for c in range(chains):
        scores(c, 0, 0, step=jnp.minimum(qi + 1, last))
    for c in range(chains):
        o_ref[c * tk:(c + 1) * tk, :] = (acc_ref[c, :, 0:MLA_V] / acc_ref[c, :, MLA_V:]).astype(o_ref.dtype)


def _causal_attention(q, k, k_rope, v, tq):
    batch, heads, seq, _ = q.shape
    nq = seq // tq
    tk = tq // 2
    return pl.pallas_call(
        _causal_attention_kernel,
        grid=(batch, heads, nq),
        in_specs=[pl.BlockSpec((1, 1, seq, MLA_QK_PAD), lambda b, h, i: (b, h, 0, 0)),
                  pl.BlockSpec((1, 1, seq, MLA_NOPE), lambda b, h, i: (b, h, 0, 0)),
                  pl.BlockSpec((1, seq, LANES), lambda b, h, i: (b, 0, 0)),
                  pl.BlockSpec((1, 1, seq, MLA_V), lambda b, h, i: (b, h, 0, 0))],
        out_specs=pl.BlockSpec((tq, MLA_V), lambda b, h, i: (b * nq + i, h)),
        out_shape=jax.ShapeDtypeStruct((batch * seq, heads * MLA_V), BF16),
        scratch_shapes=[pltpu.VMEM((2, 2, tk, tk), F32),
                        pltpu.VMEM((2, tk, LANES), F32),
                        pltpu.VMEM((2, tk, 2 * MLA_V), F32)],
        compiler_params=_cparams(("parallel", "parallel", "arbitrary")),
        name="causal_attention",
    )(q, k, k_rope, v)


def _odd_tail_kernel(a_ref, p_ref, k_ref, v_ref, x_ref, w_ref, fw_ref, out_ref):
    y_c = a_ref[...].astype(F32) * _silu(p_ref[:, _O_ZC:_O_ZC + MLA_WIDTH].astype(F32))
    acc = x_ref[...] + _dot(y_c.astype(BF16), w_ref[0:MLA_WIDTH, :])
    y_m = _mem_attention(p_ref[:, _O_MQ1:_O_MQ1 + MEM_WIDTH].astype(F32),
                         p_ref[:, _O_MZ1:_O_MZ1 + MEM_WIDTH].astype(F32), k_ref[0], v_ref[0])
    y_m = jnp.concatenate(y_m, axis=1).astype(BF16)
    acc = acc + _dot(y_m, w_ref[MLA_WIDTH:ODD_OUT, :])
    out_ref[...] = _rms(acc, fw_ref[...])


def _odd_tail(attn, proj, mem_kv, x, w_out, final_w, batch, tile):
    n = x.shape[0]
    tiles = n // batch // tile
    const = lambda shape: pl.BlockSpec(shape, lambda b, t: (0,) * len(shape))
    return pl.pallas_call(
        _odd_tail_kernel,
        grid=(batch, tiles),
        in_specs=[pl.BlockSpec((tile, MLA_WIDTH), lambda b, t: (b * tiles + t, 0)),
                  pl.BlockSpec((tile, ODD_MAIN), lambda b, t: (b * tiles + t, 0)),
                  pl.BlockSpec((1, N_MEM, MEM_WIDTH), lambda b, t: (b, 0, 2)),
                  pl.BlockSpec((1, N_MEM, MEM_WIDTH), lambda b, t: (b, 0, 3)),
                  pl.BlockSpec((tile, D_MODEL), lambda b, t: (b * tiles + t, 0)),
                  _resident((ODD_OUT, D_MODEL)), const((1, D_MODEL))],
        out_specs=pl.BlockSpec((tile, D_MODEL), lambda b, t: (b * tiles + t, 0)),
        out_shape=jax.ShapeDtypeStruct((n, D_MODEL), F32),
        compiler_params=_cparams(("parallel", "parallel")),
        name="odd_tail",
    )(attn, proj, mem_kv, mem_kv, x, w_out, final_w)


def _pad_cols(w, width):
    return jnp.pad(w, ((0, 0), (0, width - w.shape[1])))


def _expansion_matrix(width_per_head):
    e = np.zeros((LANES, SSD_HEADS * width_per_head), np.float32)
    for h in range(SSD_HEADS):
        e[h, h * width_per_head:(h + 1) * width_per_head] = 1.0
    return jnp.asarray(e, BF16)


def _shift_matrix():
    s = np.zeros(((SSD_CONV - 1) * SSD_CHUNK, 2 * SSD_CHUNK), np.float32)
    for k in range(SSD_CONV - 1):
        for t in range(SSD_CHUNK):
            s[k * SSD_CHUNK + t, SSD_CHUNK + t - (SSD_CONV - 1 - k)] = 1.0
    return jnp.asarray(s, BF16)


def kernel(x, mem, positions, mem_norm_w, norm0_w, w_in0, sc_conv_w, ssd_conv_w, ssd_conv_b, ssd_dt_bias,
           ssd_a_log, ssd_d, ssd_norm_w, mem_k0, mem_v0, w_out0, norm1_w, w_in1, mla_q_norm_w,
           mla_kv_norm_w, mla_w_uq, mla_w_ukv, mem_k1, mem_v1, w_out1, final_norm_w):
    batch, seq, d = x.shape
    n = batch * seq
    x2d = x.reshape(n, d)
    row = lambda v: v.reshape(1, -1).astype(F32)

    mem_kv = _mem_kv(mem, mem_norm_w, jnp.concatenate([mem_k0, mem_v0, mem_k1, mem_v1], axis=1).astype(BF16))

    o_dt = _O_MQ0
    w_in0 = w_in0.astype(BF16)
    w0_main = jnp.concatenate([w_in0[:, :o_dt], w_in0[:, o_dt + SSD_HEADS:]], axis=1)
    w0_dt = _pad_cols(w_in0[:, o_dt:o_dt + SSD_HEADS], LANES)
    proj0, dt_raw = _norm_matmul(x2d, norm0_w, w0_main, w0_dt, tm=min(1024, n), tn=2048)
    pad_heads = lambda v: _pad_cols(row(v), LANES)
    y0 = _even_mixer(proj0, dt_raw, mem_kv, sc_conv_w, ssd_conv_w, row(ssd_conv_b), pad_heads(ssd_dt_bias),
                     pad_heads(ssd_a_log), row(jnp.repeat(ssd_d, SSD_HEAD_DIM)), row(ssd_norm_w),
                     _expansion_matrix(SSD_HEAD_DIM), _expansion_matrix(SSD_CHUNK),
                     jnp.asarray(np.tril(np.ones((SSD_CHUNK, SSD_CHUNK), np.float32)), BF16),
                     _shift_matrix(),
                     batch, tile=min(256, seq))
    x1 = _residual_matmul(y0, w_out0.astype(BF16), x2d, tm=min(512, n))

    o_kr = MLA_Q_RANK + MLA_KV_RANK
    w_in1 = w_in1.astype(BF16)
    w1_main = jnp.concatenate([w_in1[:, :o_kr], w_in1[:, o_kr + MLA_ROPE:]], axis=1)
    w1_kr = _pad_cols(w_in1[:, o_kr:o_kr + MLA_ROPE], LANES)
    proj1, kr = _norm_matmul(x1, norm1_w, w1_main, w1_kr, tm=min(1024, n), tn=ODD_MAIN // 2)
    wq = mla_w_uq.astype(BF16).reshape(MLA_Q_RANK, MLA_HEADS, MLA_NOPE + MLA_ROPE)
    wq_rope = jnp.pad(wq[:, :, MLA_NOPE:], ((0, 0), (0, 0), (0, LANES - MLA_ROPE)))
    wq = jnp.concatenate([wq[:, :, :MLA_NOPE].reshape(MLA_Q_RANK, MLA_WIDTH),
                          wq_rope.reshape(MLA_Q_RANK, MLA_HEADS * LANES)], axis=1)
    wkv = mla_w_ukv.astype(BF16).reshape(MLA_KV_RANK, MLA_HEADS, MLA_NOPE + MLA_V)
    wkv = jnp.concatenate([wkv[:, :, :MLA_NOPE].reshape(MLA_KV_RANK, MLA_WIDTH),
                           wkv[:, :, MLA_NOPE:].reshape(MLA_KV_RANK, MLA_WIDTH)], axis=1)
    half = MLA_ROPE // 2
    inv = ROPE_THETA ** (-jnp.arange(half, dtype=F32) / half)
    freq = jnp.concatenate([inv, inv, jnp.zeros((LANES - MLA_ROPE,), F32)]).reshape(1, LANES)
    sign = np.zeros((1, LANES), np.float32)
    sign[0, :half] = -1.0
    sign[0, half:MLA_ROPE] = 1.0
    q, k, k_rope, v = _mla_qkv(proj1, kr, positions.reshape(n, 1), row(mla_q_norm_w), row(mla_kv_norm_w), wq, wkv,
                       freq, jnp.asarray(sign), batch, tile=min(512, seq))
    attn = _causal_attention(q, k, k_rope, v, tq=min(1024, seq))
    out = _odd_tail(attn, proj1, mem_kv, x1, w_out1.astype(BF16), row(final_norm_w), batch, tile=min(512, seq))
    return out.reshape(batch, seq, d)
```

```python
import functools

import numpy as np
import jax
import jax.numpy as jnp
from jax import lax
from jax.experimental import pallas as pl
from jax.experimental.pallas import tpu as pltpu

F32 = jnp.float32
BF16 = jnp.bfloat16

D_MODEL = 2048
N_MEM = 256
EPS = 1e-6
SC_WIDTH = 1024
SC_KERNEL = 3
SSD_HEADS = 32
SSD_HEAD_DIM = 64
SSD_WIDTH = SSD_HEADS * SSD_HEAD_DIM
SSD_STATE = 128
SSD_GROUPS = 4
SSD_CONV = 4
SSD_CHUNK = 128
SSD_CONV_DIM = SSD_WIDTH + 2 * SSD_GROUPS * SSD_STATE
SSD_GROUP_WIDTH = SSD_WIDTH // SSD_GROUPS
MLA_HEADS = 16
MLA_Q_RANK = 768
MLA_KV_RANK = 512
MLA_NOPE = 128
MLA_ROPE = 64
MLA_V = 128
MLA_WIDTH = MLA_HEADS * MLA_V
MLA_QK_PAD = 256
ROPE_THETA = 10000.0
MEM_HEADS = 4
MEM_HEAD_DIM = 128
MEM_WIDTH = MEM_HEADS * MEM_HEAD_DIM
EVEN_MAIN = 4 * SC_WIDTH + SSD_WIDTH + SSD_CONV_DIM + 2 * MEM_WIDTH
EVEN_OUT = SC_WIDTH + SSD_WIDTH + MEM_WIDTH
ODD_MAIN = MLA_Q_RANK + MLA_KV_RANK + MLA_WIDTH + 2 * MEM_WIDTH
ODD_OUT = MLA_WIDTH + MEM_WIDTH
LANES = 128
HALO = 8

VMEM_LIMIT = 56 * 1024 * 1024


def _cparams(sem):
    return pltpu.CompilerParams(dimension_semantics=sem, vmem_limit_bytes=VMEM_LIMIT)


def _dot(a, b):
    return jnp.dot(a, b, preferred_element_type=F32)


def _dot_nt(a, b):
    return lax.dot_general(a, b, (((1,), (1,)), ((), ())), preferred_element_type=F32)


def _rms(x, w):
    return x * lax.rsqrt(jnp.mean(x * x, axis=-1, keepdims=True) + EPS) * w


def _silu(z):
    h = 0.5 * z
    return h + h * jnp.tanh(h)


def _split_bf16(v, n):
    parts = []
    r = v
    for _ in range(n):
        p = r.astype(BF16)
        parts.append(p)
        r = r - p.astype(F32)
    return parts


def _expand(pieces, e):
    out = None
    for p in pieces:
        t = _dot(p, e)
        out = t if out is None else out + t
    return out


def _mem_kv_kernel(mem_ref, nw_ref, w_ref, out_ref):
    m = _rms(mem_ref[0], nw_ref[...])
    out_ref[0] = _dot(m.astype(BF16), w_ref[...]).astype(BF16)


def _mem_kv(mem, mem_norm_w, w_kv):
    b, m, d = mem.shape
    n = w_kv.shape[1]
    tn = MEM_WIDTH
    return pl.pallas_call(
        _mem_kv_kernel,
        grid=(b, n // tn),
        in_specs=[pl.BlockSpec((1, m, d), lambda i, j: (i, 0, 0)),
                  pl.BlockSpec((1, d), lambda i, j: (0, 0)),
                  pl.BlockSpec((d, tn), lambda i, j: (0, j))],
        out_specs=pl.BlockSpec((1, m, tn), lambda i, j: (i, 0, j)),
        out_shape=jax.ShapeDtypeStruct((b, m, n), BF16),
        compiler_params=_cparams(("parallel", "arbitrary")),
        name="mem_kv",
    )(mem, mem_norm_w.reshape(1, d), w_kv)


def _norm_matmul_kernel(x_ref, nw_ref, w_ref, ws_ref, out_ref, side_ref, xn_ref):
    @pl.when(pl.program_id(1) == 0)
    def _():
        xn_ref[...] = _rms(x_ref[...], nw_ref[...]).astype(BF16)
        side_ref[...] = _dot(xn_ref[...], ws_ref[...])

    out_ref[...] = _dot(xn_ref[...], w_ref[...]).astype(out_ref.dtype)


def _norm_matmul(x, norm_w, w, w_side, tm, tn):
    n, d = x.shape
    c = w.shape[1]
    cs = w_side.shape[1]
    return pl.pallas_call(
        _norm_matmul_kernel,
        grid=(n // tm, c // tn),
        in_specs=[pl.BlockSpec((tm, d), lambda i, j: (i, 0)),
                  pl.BlockSpec((1, d), lambda i, j: (0, 0)),
                  pl.BlockSpec((d, tn), lambda i, j: (0, j)),
                  pl.BlockSpec((d, cs), lambda i, j: (0, 0))],
        out_specs=[pl.BlockSpec((tm, tn), lambda i, j: (i, j)),
                   pl.BlockSpec((tm, cs), lambda i, j: (i, 0))],
        out_shape=[jax.ShapeDtypeStruct((n, c), BF16),
                   jax.ShapeDtypeStruct((n, cs), F32)],
        scratch_shapes=[pltpu.VMEM((tm, d), BF16)],
        compiler_params=_cparams(("parallel", "arbitrary")),
        name="norm_matmul",
    )(x, norm_w.reshape(1, d), w, w_side)


def _mem_attention(mq, mz, k, v):
    outs = []
    for h in range(MEM_HEADS):
        sl = slice(h * MEM_HEAD_DIM, (h + 1) * MEM_HEAD_DIM)
        q = (mq[:, sl] * (MEM_HEAD_DIM ** -0.5)).astype(BF16)
        s = _dot_nt(q, k[:, sl])
        p = jnp.exp(s - jnp.max(s, axis=-1, keepdims=True))
        o = _dot(p.astype(BF16), v[:, sl]) * (1.0 / jnp.sum(p, axis=-1, keepdims=True))
        outs.append(o * _silu(mz[:, sl]))
    return outs


_O_SCB, _O_SCC, _O_SCV, _O_SCZ = 0, SC_WIDTH, 2 * SC_WIDTH, 3 * SC_WIDTH
_O_SSDZ = 4 * SC_WIDTH
_O_XBC = _O_SSDZ + SSD_WIDTH
_O_MQ0 = _O_XBC + SSD_CONV_DIM
_O_MZ0 = _O_MQ0 + MEM_WIDTH


def _even_mixer_kernel(p_ref, dt_ref, k_ref, v_ref, *refs):
    consts = refs[:-4]
    y_ref, cv_buf, xbc_tail, state = refs[-4:]

    @pl.when(pl.program_id(1) == 0)
    def _():
        cv_buf[:, 0:HALO, :] = jnp.zeros((cv_buf.shape[0], HALO, SC_WIDTH), F32)
        xbc_tail[...] = jnp.zeros(xbc_tail.shape, BF16)
        state[...] = jnp.zeros(state.shape, F32)

    live = [_even_mixer_tile(p_ref.at[i], dt_ref.at[i], k_ref.at[i], v_ref.at[i], *consts,
                             y_ref.at[i], cv_buf.at[i], xbc_tail.at[i], state.at[i])
            for i in range(p_ref.shape[0])]
    while live:
        live = [g for g in live if next(g, "done") != "done"]


def _even_mixer_tile(p_ref, dt_ref, k_ref, v_ref, scw_ref, xw_ref, xb_ref, dtb_ref, alog_ref, d_ref,
                     nw_ref, ehp_ref, ehk_ref, tri_ref, shift_ref, y_ref, cv_buf, xbc_tail, state):
    tile = p_ref.shape[0]

    cv_buf[HALO:HALO + tile, :] = (p_ref[:, _O_SCC:_O_SCC + SC_WIDTH].astype(F32)
                                   * p_ref[:, _O_SCV:_O_SCV + SC_WIDTH].astype(F32))
    conv = None
    for k in range(SC_KERNEL):
        off = HALO - (SC_KERNEL - 1) + k
        t = scw_ref[k:k + 1, :] * cv_buf[off:off + tile, :]
        conv = t if conv is None else conv + t
    y_a = (p_ref[:, _O_SCB:_O_SCB + SC_WIDTH].astype(F32) * conv
           * _silu(p_ref[:, _O_SCZ:_O_SCZ + SC_WIDTH].astype(F32)))
    y_ref[:, 0:SC_WIDTH] = y_a.astype(BF16)
    cv_buf[0:HALO, :] = cv_buf[tile:tile + HALO, :]
    yield

    y_m = _mem_attention(p_ref[:, _O_MQ0:_O_MQ0 + MEM_WIDTH].astype(F32),
                         p_ref[:, _O_MZ0:_O_MZ0 + MEM_WIDTH].astype(F32), k_ref[...], v_ref[...])
    for h in range(MEM_HEADS):
        c0 = SC_WIDTH + SSD_WIDTH + h * MEM_HEAD_DIM
        y_ref[:, c0:c0 + MEM_HEAD_DIM] = y_m[h].astype(BF16)
    yield

    xbc_cols = slice(_O_XBC, _O_XBC + SSD_CONV_DIM)
    a_row = -jnp.exp(alog_ref[...])
    row = lax.broadcasted_iota(jnp.int32, (SSD_CHUNK, SSD_CHUNK), 0)
    col = lax.broadcasted_iota(jnp.int32, (SSD_CHUNK, SSD_CHUNK), 1)
    causal = row >= col
    first_head = col < SSD_HEAD_DIM
    heads_per_group = SSD_HEADS // SSD_GROUPS
    for c in range(tile // SSD_CHUNK):
        r0 = c * SSD_CHUNK
        cur = p_ref[r0:r0 + SSD_CHUNK, xbc_cols]
        prev = xbc_tail[...] if c == 0 else p_ref[r0 - SSD_CHUNK:r0, xbc_cols]
        shifted = _dot(shift_ref[...], jnp.concatenate([prev, cur], axis=0))
        u = xb_ref[...] + xw_ref[SSD_CONV - 1:SSD_CONV, :] * cur.astype(F32)
        for k in range(SSD_CONV - 1):
            u = u + xw_ref[k:k + 1, :] * shifted[k * SSD_CHUNK:(k + 1) * SSD_CHUNK, :]
        xbc = _silu(u)
        yield
        xs = xbc[:, 0:SSD_WIDTH]
        bm = xbc[:, SSD_WIDTH:SSD_WIDTH + SSD_GROUPS * SSD_STATE]
        cm = xbc[:, SSD_WIDTH + SSD_GROUPS * SSD_STATE:]
        dt_in = dt_ref[r0:r0 + SSD_CHUNK, :] + dtb_ref[...]
        dt = jnp.maximum(dt_in, 0.0) + jnp.log1p(jnp.exp(-jnp.abs(dt_in)))
        dta = dt * a_row
        tri = tri_ref[...]
        a_cum = None
        for piece in _split_bf16(dta, 3):
            t = _dot(tri, piece)
            a_cum = t if a_cum is None else a_cum + t
        a_cum_t = a_cum.T
        a_last = a_cum[SSD_CHUNK - 1:SSD_CHUNK, :]
        dt_pieces = _split_bf16(dt, 2)
        exp_a_pieces = _split_bf16(jnp.exp(a_cum), 2)
        dtd_pieces = _split_bf16(dt * jnp.exp(a_last - a_cum), 2)
        a_cum_pieces = _split_bf16(a_cum, 3)
        yield
        for g in range(SSD_GROUPS):
            gs = slice(g * SSD_GROUP_WIDTH, (g + 1) * SSD_GROUP_WIDTH)
            e_hp = ehp_ref[:, gs]
            dt_e = _expand(dt_pieces, e_hp)
            exp_a_e = _expand(exp_a_pieces, e_hp)
            dtd_e = _expand(dtd_pieces, e_hp)
            hk = slice(g * heads_per_group * SSD_CHUNK, (g + 1) * heads_per_group * SSD_CHUNK)
            a_q = _expand(a_cum_pieces, ehk_ref[:, hk])
            yield
            xs_g = xs[:, gs]
            xdt = xs_g * dt_e
            bm_g = bm[:, g * SSD_STATE:(g + 1) * SSD_STATE]
            cm_g = cm[:, g * SSD_STATE:(g + 1) * SSD_STATE].astype(BF16)
            cb = _dot_nt(cm_g, bm_g.astype(BF16))
            s_enter = state[g]
            y_off = _dot(cm_g, s_enter.astype(BF16)) * exp_a_e
            state[g] = (s_enter * exp_a_e[SSD_CHUNK - 1:SSD_CHUNK, :]
                        + _dot(bm_g.T.astype(BF16), (xs_g * dtd_e).astype(BF16)))
            yield
            pairs = []
            for j in range(heads_per_group // 2):
                lmat = []
                for h in (2 * j, 2 * j + 1):
                    h_abs = g * heads_per_group + h
                    seg = a_q[:, h * SSD_CHUNK:(h + 1) * SSD_CHUNK] - a_cum_t[h_abs:h_abs + 1, :]
                    lmat.append((cb * jnp.exp(jnp.where(causal, seg, -jnp.inf))).astype(BF16))
                xp = xdt[:, 2 * j * SSD_HEAD_DIM:(2 * j + 2) * SSD_HEAD_DIM]
                rhs = jnp.concatenate([jnp.where(first_head, xp, 0.0), jnp.where(first_head, 0.0, xp)],
                                      axis=0).astype(BF16)
                pairs.append(_dot(jnp.concatenate(lmat, axis=1), rhs))
                yield
            z = p_ref[r0:r0 + SSD_CHUNK, _O_SSDZ + gs.start:_O_SSDZ + gs.stop].astype(F32)
            y_g = (jnp.concatenate(pairs, axis=1) + y_off + xs_g * d_ref[:, gs]) * _silu(z)
            y_g = y_g * lax.rsqrt(jnp.mean(y_g * y_g, axis=-1, keepdims=True) + EPS) * nw_ref[:, gs]
            c0 = SC_WIDTH + g * SSD_GROUP_WIDTH
            y_ref[r0:r0 + SSD_CHUNK, c0:c0 + SSD_GROUP_WIDTH] = y_g.astype(BF16)
            yield
    xbc_tail[...] = p_ref[tile - SSD_CHUNK:tile, xbc_cols]


def _even_mixer(proj, dt_raw, mem_kv, sc_conv_w, ssd_conv_w, ssd_conv_b, dt_bias, a_log, d_e, ssd_norm_w,
                e_hp, e_hk, tri, shift, batch, tile):
    n = proj.shape[0]
    seq = n // batch
    rows = 2 if batch % 2 == 0 else 1
    const = lambda shape: pl.BlockSpec(shape, lambda b, t: (0,) * len(shape))
    tokens = lambda width: pl.BlockSpec((rows, tile, width), lambda b, t: (b, t, 0))
    y = pl.pallas_call(
        _even_mixer_kernel,
        grid=(batch // rows, seq // tile),
        in_specs=[tokens(EVEN_MAIN), tokens(LANES),
                  pl.BlockSpec((rows, N_MEM, MEM_WIDTH), lambda b, t: (b, 0, 0)),
                  pl.BlockSpec((rows, N_MEM, MEM_WIDTH), lambda b, t: (b, 0, 1)),
                  const((SC_KERNEL, SC_WIDTH)), const((SSD_CONV, SSD_CONV_DIM)), const((1, SSD_CONV_DIM)),
                  const((1, LANES)), const((1, LANES)), const((1, SSD_WIDTH)), const((1, SSD_WIDTH)),
                  const((LANES, SSD_WIDTH)), const((LANES, SSD_HEADS * SSD_CHUNK)),
                  const((SSD_CHUNK, SSD_CHUNK)), const(((SSD_CONV - 1) * SSD_CHUNK, 2 * SSD_CHUNK))],
        out_specs=tokens(EVEN_OUT),
        out_shape=jax.ShapeDtypeStruct((batch, seq, EVEN_OUT), BF16),
        scratch_shapes=[pltpu.VMEM((rows, tile + HALO, SC_WIDTH), F32),
                        pltpu.VMEM((rows, SSD_CHUNK, SSD_CONV_DIM), BF16),
                        pltpu.VMEM((rows, SSD_GROUPS, SSD_STATE, SSD_GROUP_WIDTH), F32)],
        compiler_params=_cparams(("parallel", "arbitrary")),
        name="even_mixer",
    )(proj.reshape(batch, seq, EVEN_MAIN), dt_raw.reshape(batch, seq, LANES), mem_kv, mem_kv, sc_conv_w,
      ssd_conv_w, ssd_conv_b, dt_bias, a_log, d_e, ssd_norm_w, e_hp, e_hk, tri, shift)
    return y.reshape(n, EVEN_OUT)


def _residual_matmul_kernel(y_ref, w_ref, x_ref, out_ref):
    out_ref[...] = x_ref[...] + _dot(y_ref[...], w_ref[...])


def _resident(shape):
    return pl.BlockSpec(shape, lambda *_: (0,) * len(shape), pipeline_mode=pl.Buffered(1))


def _residual_matmul(y, w, x, tm):
    n, k = y.shape
    d = w.shape[1]
    return pl.pallas_call(
        _residual_matmul_kernel,
        grid=(n // tm,),
        in_specs=[pl.BlockSpec((tm, k), lambda i: (i, 0)),
                  _resident((k, d)),
                  pl.BlockSpec((tm, d), lambda i: (i, 0))],
        out_specs=pl.BlockSpec((tm, d), lambda i: (i, 0)),
        out_shape=jax.ShapeDtypeStruct((n, d), F32),
        compiler_params=_cparams(("parallel",)),
        name="residual_matmul",
    )(y, w, x)


_O_CQ, _O_CKV = 0, MLA_Q_RANK
_O_ZC = MLA_Q_RANK + MLA_KV_RANK
_O_MQ1 = _O_ZC + MLA_WIDTH
_O_MZ1 = _O_MQ1 + MEM_WIDTH


def _rope_lanes(a, cos_t, sin_t, low_half):
    half = MLA_ROPE // 2
    swapped = jnp.where(low_half, pltpu.roll(a, LANES - half, 1), pltpu.roll(a, half, 1))
    return a * cos_t + swapped * sin_t


def _mla_qkv_kernel(p_ref, kr_ref, pos_ref, qnw_ref, kvnw_ref, wq_ref, wkt_ref, wv_ref, freq_ref, sign_ref,
                    q_ref, k_ref, kro_ref, v_ref):
    tile = p_ref.shape[0]
    cq = _rms(p_ref[:, _O_CQ:_O_CQ + MLA_Q_RANK].astype(F32), qnw_ref[...])
    q = _dot(cq.astype(BF16), wq_ref[...])
    ckv = _rms(p_ref[:, _O_CKV:_O_CKV + MLA_KV_RANK].astype(F32), kvnw_ref[...]).astype(BF16)
    kt = _dot_nt(wkt_ref[...], ckv)
    v = _dot(ckv, wv_ref[...])
    ang = pos_ref[...].astype(F32) * freq_ref[...]
    cos_t = jnp.cos(ang)
    sin_t = jnp.sin(ang) * sign_ref[...]
    low_half = lax.broadcasted_iota(jnp.int32, (tile, LANES), 1) < MLA_ROPE // 2
    kro_ref[0, 0] = _rope_lanes(kr_ref[...], cos_t, sin_t, low_half).T.astype(BF16)
    scale = (MLA_NOPE + MLA_ROPE) ** -0.5 * np.log2(np.e)
    for h in range(MLA_HEADS):
        nope = slice(h * MLA_NOPE, (h + 1) * MLA_NOPE)
        pair = q[:, MLA_WIDTH + (h // 2) * LANES:MLA_WIDTH + (h // 2 + 1) * LANES]
        a = pair if h % 2 == 0 else pltpu.roll(pair, MLA_ROPE, 1)
        q_ref[0, h, :, 0:MLA_NOPE] = (q[:, nope] * scale).astype(BF16)
        q_ref[0, h, :, MLA_NOPE:MLA_QK_PAD] = (_rope_lanes(a, cos_t, sin_t, low_half) * scale).astype(BF16)
        k_ref[0, h, 0] = kt[nope, :].astype(BF16)
        v_ref[0, h] = v[:, h * MLA_V:(h + 1) * MLA_V].astype(BF16)


def _mla_qkv(proj, kr, pos, q_norm_w, kv_norm_w, w_uq, w_uk_t, w_uv, freq, sign, batch, tile):
    n = proj.shape[0]
    seq = n // batch
    tiles = seq // tile
    const = lambda shape: pl.BlockSpec(shape, lambda b, t: (0,) * len(shape))
    head_spec = lambda w: pl.BlockSpec((1, MLA_HEADS, tile, w), lambda b, t: (b, 0, t, 0))
    return pl.pallas_call(
        _mla_qkv_kernel,
        grid=(batch, tiles),
        in_specs=[pl.BlockSpec((tile, MLA_Q_RANK + MLA_KV_RANK), lambda b, t: (b * tiles + t, 0)),
                  pl.BlockSpec((tile, LANES), lambda b, t: (b * tiles + t, 0)),
                  pl.BlockSpec((tile, 1), lambda b, t: (b * tiles + t, 0)),
                  const((1, MLA_Q_RANK)), const((1, MLA_KV_RANK)),
                  const((MLA_Q_RANK, MLA_WIDTH + MLA_HEADS * MLA_ROPE)),
                  const((MLA_WIDTH, MLA_KV_RANK)), const((MLA_KV_RANK, MLA_WIDTH)),
                  const((1, LANES)), const((1, LANES))],
        out_specs=[head_spec(MLA_QK_PAD),
                   pl.BlockSpec((1, MLA_HEADS, 1, MLA_NOPE, tile), lambda b, t: (b, 0, t, 0, 0)),
                   pl.BlockSpec((1, 1, LANES, tile), lambda b, t: (b, t, 0, 0)), head_spec(MLA_V)],
        out_shape=[jax.ShapeDtypeStruct((batch, MLA_HEADS, seq, MLA_QK_PAD), BF16),
                   jax.ShapeDtypeStruct((batch, MLA_HEADS, tiles, MLA_NOPE, tile), BF16),
                   jax.ShapeDtypeStruct((batch, tiles, LANES, tile), BF16),
                   jax.ShapeDtypeStruct((batch, MLA_HEADS, seq, MLA_V), BF16)],
        compiler_params=_cparams(("parallel", "parallel")),
        name="mla_qkv",
    )(proj, kr, pos, q_norm_w, kv_norm_w, w_uq, w_uk_t, w_uv, freq, sign)


def _causal_attention_kernel(q_ref, k_ref, kr_ref, v_ref, o_ref, s_ref, m_ref, acc_ref, *, tq):
    chains = 2
    heads = range(q_ref.shape[1])
    tk = tq // chains
    tiles = q_ref.shape[2] // tq
    ones = jnp.ones((tk, MLA_V), BF16)

    def kv_rows(j):
        return pl.ds(pl.multiple_of(j * tk, tk), tk)

    def scores(hd, c, j, buf, tile):
        k_t = jnp.concatenate([k_ref[0, hd, j], kr_ref[0, j]], axis=0)
        s_ref[buf, hd, c] = _dot(q_ref[0, hd, kv_rows(tile * chains + c), :], k_t)

    def softmax_pv(hd, c, j, buf, diagonal):
        s = s_ref[buf, hd, c]
        if diagonal:
            row = lax.broadcasted_iota(jnp.int32, (tk, tk), 0)
            col = lax.broadcasted_iota(jnp.int32, (tk, tk), 1)
            s = jnp.where(col <= row, s, -1e30)
        m = m_ref[hd, c]
        m_new = jnp.maximum(m, jnp.max(s, axis=-1, keepdims=True))
        m_ref[hd, c] = m_new
        p = jnp.exp2(s - jnp.concatenate([m_new] * (tk // LANES), axis=1)).astype(BF16)
        alpha = jnp.exp2(m - m_new)
        pv = _dot(p, jnp.concatenate([v_ref[0, hd, kv_rows(j), :], ones], axis=1))
        for half in range(2):
            lanes = slice(half * MLA_V, (half + 1) * MLA_V)
            acc_ref[hd, c, :, lanes] = alpha * acc_ref[hd, c, :, lanes] + pv[:, lanes]

    for hd in heads:
        for c in range(chains):
            scores(hd, c, 0, 0, tile=0)

    def query_tile(qi, carry):
        m_ref[...] = jnp.full(m_ref.shape, -1e30, F32)
        acc_ref[...] = jnp.zeros(acc_ref.shape, F32)

        def pairs(count):
            def body(jj, carry):
                for u in range(count):
                    j = 2 * (count * jj + u)
                    for cur in range(2):
                        for hd in heads:
                            for c in range(chains):
                                scores(hd, c, j + cur + 1, 1 - cur, tile=qi)
                        for hd in heads:
                            for c in range(chains):
                                softmax_pv(hd, c, j + cur, cur, False)
                return carry
            return body

        lax.fori_loop(0, qi // 4, pairs(4), 0)
        lax.fori_loop(qi // 4 * 2, qi // 2, pairs(2), 0)
        lax.fori_loop(qi // 2 * 2, qi, pairs(1), 0)
        j = 2 * qi
        for hd in heads:
            scores(hd, 1, j + 1, 1, tile=qi)
        for hd in heads:
            softmax_pv(hd, 0, j, 0, True)
        for hd in heads:
            softmax_pv(hd, 1, j, 0, False)
        for hd in heads:
            for c in range(chains):
                scores(hd, c, 0, 0, tile=jnp.minimum(qi + 1, tiles - 1))
        for hd in heads:
            softmax_pv(hd, 1, j + 1, 1, True)
        for hd in heads:
            for c in range(chains):
                o_ref[kv_rows(qi * chains + c), hd * MLA_V:(hd + 1) * MLA_V] = (
                    acc_ref[hd, c, :, 0:MLA_V] / acc_ref[hd, c, :, MLA_V:]).astype(o_ref.dtype)
        return carry

    lax.fori_loop(0, tiles, query_tile, 0)


def _causal_attention(q, k, k_rope, v, tq, hp):
    batch, heads, seq, _ = q.shape
    tk = tq // 2
    per_head = lambda width: pl.BlockSpec((1, hp, seq, width), lambda b, h: (b, h, 0, 0))
    return pl.pallas_call(
        functools.partial(_causal_attention_kernel, tq=tq),
        grid=(batch, heads // hp),
        in_specs=[per_head(MLA_QK_PAD),
                  pl.BlockSpec((1, hp, seq // tk, MLA_NOPE, tk), lambda b, h: (b, h, 0, 0, 0)),
                  pl.BlockSpec((1, seq // tk, LANES, tk), lambda b, h: (b, 0, 0, 0)), per_head(MLA_V)],
        out_specs=pl.BlockSpec((seq, hp * MLA_V), lambda b, h: (b, h)),
        out_shape=jax.ShapeDtypeStruct((batch * seq, heads * MLA_V), BF16),
        scratch_shapes=[pltpu.VMEM((2, hp, 2, tk, tk), F32),
                        pltpu.VMEM((hp, 2, tk, LANES), F32),
                        pltpu.VMEM((hp, 2, tk, 2 * MLA_V), F32)],
        compiler_params=_cparams(("parallel", "parallel")),
        name="causal_attention",
    )(q, k, k_rope, v)


def _odd_tail_kernel(a_ref, p_ref, k_ref, v_ref, x_ref, w_ref, fw_ref, out_ref):
    y_c = a_ref[...].astype(F32) * _silu(p_ref[:, _O_ZC:_O_ZC + MLA_WIDTH].astype(F32))
    acc = x_ref[...] + _dot(y_c.astype(BF16), w_ref[0:MLA_WIDTH, :])
    y_m = _mem_attention(p_ref[:, _O_MQ1:_O_MQ1 + MEM_WIDTH].astype(F32),
                         p_ref[:, _O_MZ1:_O_MZ1 + MEM_WIDTH].astype(F32), k_ref[0], v_ref[0])
    y_m = jnp.concatenate(y_m, axis=1).astype(BF16)
    acc = acc + _dot(y_m, w_ref[MLA_WIDTH:ODD_OUT, :])
    out_ref[...] = _rms(acc, fw_ref[...])


def _odd_tail(attn, proj, mem_kv, x, w_out, final_w, batch, tile):
    n = x.shape[0]
    tiles = n // batch // tile
    const = lambda shape: pl.BlockSpec(shape, lambda b, t: (0,) * len(shape))
    return pl.pallas_call(
        _odd_tail_kernel,
        grid=(batch, tiles),
        in_specs=[pl.BlockSpec((tile, MLA_WIDTH), lambda b, t: (b * tiles + t, 0)),
                  pl.BlockSpec((tile, ODD_MAIN), lambda b, t: (b * tiles + t, 0)),
                  pl.BlockSpec((1, N_MEM, MEM_WIDTH), lambda b, t: (b, 0, 2)),
                  pl.BlockSpec((1, N_MEM, MEM_WIDTH), lambda b, t: (b, 0, 3)),
                  pl.BlockSpec((tile, D_MODEL), lambda b, t: (b * tiles + t, 0)),
                  _resident((ODD_OUT, D_MODEL)), const((1, D_MODEL))],
        out_specs=pl.BlockSpec((tile, D_MODEL), lambda b, t: (b * tiles + t, 0)),
        out_shape=jax.ShapeDtypeStruct((n, D_MODEL), F32),
        compiler_params=_cparams(("parallel", "parallel")),
        name="odd_tail",
    )(attn, proj, mem_kv, mem_kv, x, w_out, final_w)


def _pad_cols(w, width):
    return jnp.pad(w, ((0, 0), (0, width - w.shape[1])))


def _expansion_matrix(width_per_head):
    e = np.zeros((LANES, SSD_HEADS * width_per_head), np.float32)
    for h in range(SSD_HEADS):
        e[h, h * width_per_head:(h + 1) * width_per_head] = 1.0
    return jnp.asarray(e, BF16)


def _shift_matrix():
    s = np.zeros(((SSD_CONV - 1) * SSD_CHUNK, 2 * SSD_CHUNK), np.float32)
    for k in range(SSD_CONV - 1):
        for t in range(SSD_CHUNK):
            s[k * SSD_CHUNK + t, SSD_CHUNK + t - (SSD_CONV - 1 - k)] = 1.0
    return jnp.asarray(s, BF16)


def kernel(x, mem, positions, mem_norm_w, norm0_w, w_in0, sc_conv_w, ssd_conv_w, ssd_conv_b, ssd_dt_bias,
           ssd_a_log, ssd_d, ssd_norm_w, mem_k0, mem_v0, w_out0, norm1_w, w_in1, mla_q_norm_w,
           mla_kv_norm_w, mla_w_uq, mla_w_ukv, mem_k1, mem_v1, w_out1, final_norm_w):
    batch, seq, d = x.shape
    n = batch * seq
    x2d = x.reshape(n, d)
    row = lambda v: v.reshape(1, -1).astype(F32)

    mem_kv = _mem_kv(mem, mem_norm_w, jnp.concatenate([mem_k0, mem_v0, mem_k1, mem_v1], axis=1).astype(BF16))

    o_dt = _O_MQ0
    w_in0 = w_in0.astype(BF16)
    w0_main = jnp.concatenate([w_in0[:, :o_dt], w_in0[:, o_dt + SSD_HEADS:]], axis=1)
    w0_dt = _pad_cols(w_in0[:, o_dt:o_dt + SSD_HEADS], LANES)
    proj0, dt_raw = _norm_matmul(x2d, norm0_w, w0_main, w0_dt, tm=min(1024, n), tn=2048)
    pad_heads = lambda v: _pad_cols(row(v), LANES)
    y0 = _even_mixer(proj0, dt_raw, mem_kv, sc_conv_w, ssd_conv_w, row(ssd_conv_b), pad_heads(ssd_dt_bias),
                     pad_heads(ssd_a_log), row(jnp.repeat(ssd_d, SSD_HEAD_DIM)), row(ssd_norm_w),
                     _expansion_matrix(SSD_HEAD_DIM), _expansion_matrix(SSD_CHUNK),
                     jnp.asarray(np.tril(np.ones((SSD_CHUNK, SSD_CHUNK), np.float32)), BF16),
                     _shift_matrix(),
                     batch, tile=min(256, seq))
    x1 = _residual_matmul(y0, w_out0.astype(BF16), x2d, tm=min(512, n))

    o_kr = MLA_Q_RANK + MLA_KV_RANK
    w_in1 = w_in1.astype(BF16)
    w1_main = jnp.concatenate([w_in1[:, :o_kr], w_in1[:, o_kr + MLA_ROPE:]], axis=1)
    w1_kr = _pad_cols(w_in1[:, o_kr:o_kr + MLA_ROPE], LANES)
    proj1, kr = _norm_matmul(x1, norm1_w, w1_main, w1_kr, tm=min(1024, n), tn=ODD_MAIN // 2)
    wq = mla_w_uq.astype(BF16).reshape(MLA_Q_RANK, MLA_HEADS, MLA_NOPE + MLA_ROPE)
    wq = jnp.concatenate([wq[:, :, :MLA_NOPE].reshape(MLA_Q_RANK, MLA_WIDTH),
                          wq[:, :, MLA_NOPE:].reshape(MLA_Q_RANK, MLA_HEADS * MLA_ROPE)], axis=1)
    wkv = mla_w_ukv.astype(BF16).reshape(MLA_KV_RANK, MLA_HEADS, MLA_NOPE + MLA_V)
    w_uk_t = wkv[:, :, :MLA_NOPE].reshape(MLA_KV_RANK, MLA_WIDTH).T
    w_uv = wkv[:, :, MLA_NOPE:].reshape(MLA_KV_RANK, MLA_WIDTH)
    half = MLA_ROPE // 2
    inv = ROPE_THETA ** (-jnp.arange(half, dtype=F32) / half)
    freq = jnp.concatenate([inv, inv, jnp.zeros((LANES - MLA_ROPE,), F32)]).reshape(1, LANES)
    sign = np.zeros((1, LANES), np.float32)
    sign[0, :half] = -1.0
    sign[0, half:MLA_ROPE] = 1.0
    q, k, k_rope, v = _mla_qkv(proj1, kr, positions.reshape(n, 1), row(mla_q_norm_w), row(mla_kv_norm_w), wq,
                               w_uk_t, w_uv, freq, jnp.asarray(sign), batch, tile=min(512, seq))
    attn = _causal_attention(q, k, k_rope, v, tq=2 * min(512, seq), hp=1)
    out = _odd_tail(attn, proj1, mem_kv, x1, w_out1.astype(BF16), row(final_norm_w), batch, tile=min(512, seq))
    return out.reshape(batch, seq, d)
```

```python
import functools

import numpy as np
import jax
import jax.numpy as jnp
from jax import lax
from jax.experimental import pallas as pl
from jax.experimental.pallas import tpu as pltpu

F32 = jnp.float32
BF16 = jnp.bfloat16

D_MODEL = 2048
N_MEM = 256
EPS = 1e-6
SC_WIDTH = 1024
SC_KERNEL = 3
SSD_HEADS = 32
SSD_HEAD_DIM = 64
SSD_WIDTH = SSD_HEADS * SSD_HEAD_DIM
SSD_STATE = 128
SSD_GROUPS = 4
SSD_CONV = 4
SSD_CHUNK = 128
SSD_CONV_DIM = SSD_WIDTH + 2 * SSD_GROUPS * SSD_STATE
SSD_GROUP_WIDTH = SSD_WIDTH // SSD_GROUPS
MLA_HEADS = 16
MLA_Q_RANK = 768
MLA_KV_RANK = 512
MLA_NOPE = 128
MLA_ROPE = 64
MLA_V = 128
MLA_WIDTH = MLA_HEADS * MLA_V
MLA_QK_PAD = 256
ROPE_THETA = 10000.0
MEM_HEADS = 4
MEM_HEAD_DIM = 128
MEM_WIDTH = MEM_HEADS * MEM_HEAD_DIM
EVEN_MAIN = 4 * SC_WIDTH + SSD_WIDTH + SSD_CONV_DIM + 2 * MEM_WIDTH
EVEN_OUT = SC_WIDTH + SSD_WIDTH + MEM_WIDTH
ODD_MAIN = MLA_Q_RANK + MLA_KV_RANK + MLA_WIDTH + 2 * MEM_WIDTH
ODD_OUT = MLA_WIDTH + MEM_WIDTH
LANES = 128
HALO = 8

VMEM_LIMIT = 56 * 1024 * 1024


def _cparams(sem):
    return pltpu.CompilerParams(dimension_semantics=sem, vmem_limit_bytes=VMEM_LIMIT)


def _dot(a, b):
    return jnp.dot(a, b, preferred_element_type=F32)


def _dot_nt(a, b):
    return lax.dot_general(a, b, (((1,), (1,)), ((), ())), preferred_element_type=F32)


def _rms(x, w):
    return x * lax.rsqrt(jnp.mean(x * x, axis=-1, keepdims=True) + EPS) * w


def _silu(z):
    h = 0.5 * z
    return h + h * jnp.tanh(h)


def _split_bf16(v, n):
    parts = []
    r = v
    for _ in range(n):
        p = r.astype(BF16)
        parts.append(p)
        r = r - p.astype(F32)
    return parts


def _expand(pieces, e):
    out = None
    for p in pieces:
        t = _dot(p, e)
        out = t if out is None else out + t
    return out


def _mem_kv_kernel(mem_ref, nw_ref, w_ref, out_ref):
    m = _rms(mem_ref[0], nw_ref[...])
    out_ref[0] = _dot(m.astype(BF16), w_ref[...]).astype(BF16)


def _mem_kv(mem, mem_norm_w, w_kv):
    b, m, d = mem.shape
    n = w_kv.shape[1]
    tn = MEM_WIDTH
    return pl.pallas_call(
        _mem_kv_kernel,
        grid=(b, n // tn),
        in_specs=[pl.BlockSpec((1, m, d), lambda i, j: (i, 0, 0)),
                  pl.BlockSpec((1, d), lambda i, j: (0, 0)),
                  pl.BlockSpec((d, tn), lambda i, j: (0, j))],
        out_specs=pl.BlockSpec((1, m, tn), lambda i, j: (i, 0, j)),
        out_shape=jax.ShapeDtypeStruct((b, m, n), BF16),
        compiler_params=_cparams(("parallel", "arbitrary")),
        name="mem_kv",
    )(mem, mem_norm_w.reshape(1, d), w_kv)


def _norm_matmul_kernel(x_ref, nw_ref, w_ref, ws_ref, out_ref, side_ref, xn_ref):
    @pl.when(pl.program_id(1) == 0)
    def _():
        xn_ref[...] = _rms(x_ref[...], nw_ref[...]).astype(BF16)
        side_ref[...] = _dot(xn_ref[...], ws_ref[...])

    out_ref[...] = _dot(xn_ref[...], w_ref[...]).astype(out_ref.dtype)


def _norm_matmul(x, norm_w, w, w_side, tm, tn):
    n, d = x.shape
    c = w.shape[1]
    cs = w_side.shape[1]
    return pl.pallas_call(
        _norm_matmul_kernel,
        grid=(n // tm, c // tn),
        in_specs=[pl.BlockSpec((tm, d), lambda i, j: (i, 0)),
                  pl.BlockSpec((1, d), lambda i, j: (0, 0)),
                  pl.BlockSpec((d, tn), lambda i, j: (0, j)),
                  pl.BlockSpec((d, cs), lambda i, j: (0, 0))],
        out_specs=[pl.BlockSpec((tm, tn), lambda i, j: (i, j)),
                   pl.BlockSpec((tm, cs), lambda i, j: (i, 0))],
        out_shape=[jax.ShapeDtypeStruct((n, c), BF16),
                   jax.ShapeDtypeStruct((n, cs), F32)],
        scratch_shapes=[pltpu.VMEM((tm, d), BF16)],
        compiler_params=_cparams(("parallel", "arbitrary")),
        name="norm_matmul",
    )(x, norm_w.reshape(1, d), w, w_side)


def _mem_attention(mq, mz, k, v):
    outs = []
    for h in range(MEM_HEADS):
        sl = slice(h * MEM_HEAD_DIM, (h + 1) * MEM_HEAD_DIM)
        q = (mq[:, sl] * (MEM_HEAD_DIM ** -0.5)).astype(BF16)
        s = _dot_nt(q, k[:, sl])
        p = jnp.exp(s - jnp.max(s, axis=-1, keepdims=True))
        o = _dot(p.astype(BF16), v[:, sl]) * (1.0 / jnp.sum(p, axis=-1, keepdims=True))
        outs.append(o * _silu(mz[:, sl]))
    return outs


_O_SCB, _O_SCC, _O_SCV, _O_SCZ = 0, SC_WIDTH, 2 * SC_WIDTH, 3 * SC_WIDTH
_O_SSDZ = 4 * SC_WIDTH
_O_XBC = _O_SSDZ + SSD_WIDTH
_O_MQ0 = _O_XBC + SSD_CONV_DIM
_O_MZ0 = _O_MQ0 + MEM_WIDTH


def _even_mixer_kernel(p_ref, dt_ref, k_ref, v_ref, *refs):
    consts = refs[:-4]
    y_ref, cv_buf, xbc_tail, state = refs[-4:]

    @pl.when(pl.program_id(1) == 0)
    def _():
        cv_buf[:, 0:HALO, :] = jnp.zeros((cv_buf.shape[0], HALO, SC_WIDTH), F32)
        xbc_tail[...] = jnp.zeros(xbc_tail.shape, BF16)
        state[...] = jnp.zeros(state.shape, F32)

    live = [_even_mixer_tile(p_ref.at[i], dt_ref.at[i], k_ref.at[i], v_ref.at[i], *consts,
                             y_ref.at[i], cv_buf.at[i], xbc_tail.at[i], state.at[i])
            for i in range(p_ref.shape[0])]
    while live:
        live = [g for g in live if next(g, "done") != "done"]


def _even_mixer_tile(p_ref, dt_ref, k_ref, v_ref, scw_ref, xw_ref, xb_ref, dtb_ref, alog_ref, d_ref,
                     nw_ref, ehp_ref, ehk_ref, tri_ref, shift_ref, y_ref, cv_buf, xbc_tail, state):
    tile = p_ref.shape[0]

    cv_buf[HALO:HALO + tile, :] = (p_ref[:, _O_SCC:_O_SCC + SC_WIDTH].astype(F32)
                                   * p_ref[:, _O_SCV:_O_SCV + SC_WIDTH].astype(F32))
    conv = None
    for k in range(SC_KERNEL):
        off = HALO - (SC_KERNEL - 1) + k
        t = scw_ref[k:k + 1, :] * cv_buf[off:off + tile, :]
        conv = t if conv is None else conv + t
    y_a = (p_ref[:, _O_SCB:_O_SCB + SC_WIDTH].astype(F32) * conv
           * _silu(p_ref[:, _O_SCZ:_O_SCZ + SC_WIDTH].astype(F32)))
    y_ref[:, 0:SC_WIDTH] = y_a.astype(BF16)
    cv_buf[0:HALO, :] = cv_buf[tile:tile + HALO, :]
    yield

    y_m = _mem_attention(p_ref[:, _O_MQ0:_O_MQ0 + MEM_WIDTH].astype(F32),
                         p_ref[:, _O_MZ0:_O_MZ0 + MEM_WIDTH].astype(F32), k_ref[...], v_ref[...])
    for h in range(MEM_HEADS):
        c0 = SC_WIDTH + SSD_WIDTH + h * MEM_HEAD_DIM
        y_ref[:, c0:c0 + MEM_HEAD_DIM] = y_m[h].astype(BF16)
    yield

    xbc_cols = slice(_O_XBC, _O_XBC + SSD_CONV_DIM)
    a_row = -jnp.exp(alog_ref[...])
    row = lax.broadcasted_iota(jnp.int32, (SSD_CHUNK, SSD_CHUNK), 0)
    col = lax.broadcasted_iota(jnp.int32, (SSD_CHUNK, SSD_CHUNK), 1)
    causal = row >= col
    first_head = col < SSD_HEAD_DIM
    heads_per_group = SSD_HEADS // SSD_GROUPS
    for c in range(tile // SSD_CHUNK):
        r0 = c * SSD_CHUNK
        cur = p_ref[r0:r0 + SSD_CHUNK, xbc_cols]
        prev = xbc_tail[...] if c == 0 else p_ref[r0 - SSD_CHUNK:r0, xbc_cols]
        shifted = _dot(shift_ref[...], jnp.concatenate([prev, cur], axis=0))
        u = xb_ref[...] + xw_ref[SSD_CONV - 1:SSD_CONV, :] * cur.astype(F32)
        for k in range(SSD_CONV - 1):
            u = u + xw_ref[k:k + 1, :] * shifted[k * SSD_CHUNK:(k + 1) * SSD_CHUNK, :]
        xbc = _silu(u)
        yield
        xs = xbc[:, 0:SSD_WIDTH]
        bm = xbc[:, SSD_WIDTH:SSD_WIDTH + SSD_GROUPS * SSD_STATE]
        cm = xbc[:, SSD_WIDTH + SSD_GROUPS * SSD_STATE:]
        dt_in = dt_ref[r0:r0 + SSD_CHUNK, :] + dtb_ref[...]
        dt = jnp.maximum(dt_in, 0.0) + jnp.log1p(jnp.exp(-jnp.abs(dt_in)))
        dta = dt * a_row
        tri = tri_ref[...]
        a_cum = None
        for piece in _split_bf16(dta, 3):
            t = _dot(tri, piece)
            a_cum = t if a_cum is None else a_cum + t
        a_cum_t = a_cum.T
        a_last = a_cum[SSD_CHUNK - 1:SSD_CHUNK, :]
        dt_pieces = _split_bf16(dt, 2)
        exp_a_pieces = _split_bf16(jnp.exp(a_cum), 2)
        dtd_pieces = _split_bf16(dt * jnp.exp(a_last - a_cum), 2)
        a_cum_pieces = _split_bf16(a_cum, 3)
        yield
        for g in range(SSD_GROUPS):
            gs = slice(g * SSD_GROUP_WIDTH, (g + 1) * SSD_GROUP_WIDTH)
            e_hp = ehp_ref[:, gs]
            dt_e = _expand(dt_pieces, e_hp)
            exp_a_e = _expand(exp_a_pieces, e_hp)
            dtd_e = _expand(dtd_pieces, e_hp)
            hk = slice(g * heads_per_group * SSD_CHUNK, (g + 1) * heads_per_group * SSD_CHUNK)
            a_q = _expand(a_cum_pieces, ehk_ref[:, hk])
            yield
            xs_g = xs[:, gs]
            xdt = xs_g * dt_e
            bm_g = bm[:, g * SSD_STATE:(g + 1) * SSD_STATE]
            cm_g = cm[:, g * SSD_STATE:(g + 1) * SSD_STATE].astype(BF16)
            cb = _dot_nt(cm_g, bm_g.astype(BF16))
            s_enter = state[g]
            y_off = _dot(cm_g, s_enter.astype(BF16)) * exp_a_e
            state[g] = (s_enter * exp_a_e[SSD_CHUNK - 1:SSD_CHUNK, :]
                        + _dot(bm_g.T.astype(BF16), (xs_g * dtd_e).astype(BF16)))
            yield
            pairs = []
            for j in range(heads_per_group // 2):
                lmat = []
                for h in (2 * j, 2 * j + 1):
                    h_abs = g * heads_per_group + h
                    seg = a_q[:, h * SSD_CHUNK:(h + 1) * SSD_CHUNK] - a_cum_t[h_abs:h_abs + 1, :]
                    lmat.append((cb * jnp.exp(jnp.where(causal, seg, -jnp.inf))).astype(BF16))
                xp = xdt[:, 2 * j * SSD_HEAD_DIM:(2 * j + 2) * SSD_HEAD_DIM]
                rhs = jnp.concatenate([jnp.where(first_head, xp, 0.0), jnp.where(first_head, 0.0, xp)],
                                      axis=0).astype(BF16)
                pairs.append(_dot(jnp.concatenate(lmat, axis=1), rhs))
                yield
            z = p_ref[r0:r0 + SSD_CHUNK, _O_SSDZ + gs.start:_O_SSDZ + gs.stop].astype(F32)
            y_g = (jnp.concatenate(pairs, axis=1) + y_off + xs_g * d_ref[:, gs]) * _silu(z)
            y_g = y_g * lax.rsqrt(jnp.mean(y_g * y_g, axis=-1, keepdims=True) + EPS) * nw_ref[:, gs]
            c0 = SC_WIDTH + g * SSD_GROUP_WIDTH
            y_ref[r0:r0 + SSD_CHUNK, c0:c0 + SSD_GROUP_WIDTH] = y_g.astype(BF16)
            yield
    xbc_tail[...] = p_ref[tile - SSD_CHUNK:tile, xbc_cols]


def _even_mixer(proj, dt_raw, mem_kv, sc_conv_w, ssd_conv_w, ssd_conv_b, dt_bias, a_log, d_e, ssd_norm_w,
                e_hp, e_hk, tri, shift, batch, tile):
    n = proj.shape[0]
    seq = n // batch
    rows = 2 if batch % 2 == 0 else 1
    const = lambda shape: pl.BlockSpec(shape, lambda b, t: (0,) * len(shape))
    tokens = lambda width: pl.BlockSpec((rows, tile, width), lambda b, t: (b, t, 0))
    y = pl.pallas_call(
        _even_mixer_kernel,
        grid=(batch // rows, seq // tile),
        in_specs=[tokens(EVEN_MAIN), tokens(LANES),
                  pl.BlockSpec((rows, N_MEM, MEM_WIDTH), lambda b, t: (b, 0, 0)),
                  pl.BlockSpec((rows, N_MEM, MEM_WIDTH), lambda b, t: (b, 0, 1)),
                  const((SC_KERNEL, SC_WIDTH)), const((SSD_CONV, SSD_CONV_DIM)), const((1, SSD_CONV_DIM)),
                  const((1, LANES)), const((1, LANES)), const((1, SSD_WIDTH)), const((1, SSD_WIDTH)),
                  const((LANES, SSD_WIDTH)), const((LANES, SSD_HEADS * SSD_CHUNK)),
                  const((SSD_CHUNK, SSD_CHUNK)), const(((SSD_CONV - 1) * SSD_CHUNK, 2 * SSD_CHUNK))],
        out_specs=tokens(EVEN_OUT),
        out_shape=jax.ShapeDtypeStruct((batch, seq, EVEN_OUT), BF16),
        scratch_shapes=[pltpu.VMEM((rows, tile + HALO, SC_WIDTH), F32),
                        pltpu.VMEM((rows, SSD_CHUNK, SSD_CONV_DIM), BF16),
                        pltpu.VMEM((rows, SSD_GROUPS, SSD_STATE, SSD_GROUP_WIDTH), F32)],
        compiler_params=_cparams(("parallel", "arbitrary")),
        name="even_mixer",
    )(proj.reshape(batch, seq, EVEN_MAIN), dt_raw.reshape(batch, seq, LANES), mem_kv, mem_kv, sc_conv_w,
      ssd_conv_w, ssd_conv_b, dt_bias, a_log, d_e, ssd_norm_w, e_hp, e_hk, tri, shift)
    return y.reshape(n, EVEN_OUT)


def _residual_matmul_kernel(y_ref, w_ref, x_ref, out_ref):
    out_ref[...] = x_ref[...] + _dot(y_ref[...], w_ref[...])


def _resident(shape):
    return pl.BlockSpec(shape, lambda *_: (0,) * len(shape), pipeline_mode=pl.Buffered(1))


def _residual_matmul(y, w, x, tm):
    n, k = y.shape
    d = w.shape[1]
    return pl.pallas_call(
        _residual_matmul_kernel,
        grid=(n // tm,),
        in_specs=[pl.BlockSpec((tm, k), lambda i: (i, 0)),
                  _resident((k, d)),
                  pl.BlockSpec((tm, d), lambda i: (i, 0))],
        out_specs=pl.BlockSpec((tm, d), lambda i: (i, 0)),
        out_shape=jax.ShapeDtypeStruct((n, d), F32),
        compiler_params=_cparams(("parallel",)),
        name="residual_matmul",
    )(y, w, x)


_O_CQ, _O_CKV = 0, MLA_Q_RANK
_O_ZC = MLA_Q_RANK + MLA_KV_RANK
_O_MQ1 = _O_ZC + MLA_WIDTH
_O_MZ1 = _O_MQ1 + MEM_WIDTH


def _rope_lanes(a, cos_t, sin_t, low_half):
    half = MLA_ROPE // 2
    swapped = jnp.where(low_half, pltpu.roll(a, LANES - half, 1), pltpu.roll(a, half, 1))
    return a * cos_t + swapped * sin_t


def _mla_qkv_kernel(p_ref, kr_ref, pos_ref, qnw_ref, kvnw_ref, wq_ref, wkt_ref, wv_ref, freq_ref, sign_ref,
                    q_ref, k_ref, kro_ref, v_ref):
    tile = p_ref.shape[0]
    cq = _rms(p_ref[:, _O_CQ:_O_CQ + MLA_Q_RANK].astype(F32), qnw_ref[...])
    q = _dot(cq.astype(BF16), wq_ref[...])
    ckv = _rms(p_ref[:, _O_CKV:_O_CKV + MLA_KV_RANK].astype(F32), kvnw_ref[...]).astype(BF16)
    kt = _dot_nt(wkt_ref[...], ckv)
    v = _dot(ckv, wv_ref[...])
    ang = pos_ref[...].astype(F32) * freq_ref[...]
    cos_t = jnp.cos(ang)
    sin_t = jnp.sin(ang) * sign_ref[...]
    low_half = lax.broadcasted_iota(jnp.int32, (tile, LANES), 1) < MLA_ROPE // 2
    kro_ref[0, 0] = _rope_lanes(kr_ref[...], cos_t, sin_t, low_half).T.astype(BF16)
    scale = (MLA_NOPE + MLA_ROPE) ** -0.5 * np.log2(np.e)
    for h in range(MLA_HEADS):
        nope = slice(h * MLA_NOPE, (h + 1) * MLA_NOPE)
        pair = q[:, MLA_WIDTH + (h // 2) * LANES:MLA_WIDTH + (h // 2 + 1) * LANES]
        a = pair if h % 2 == 0 else pltpu.roll(pair, MLA_ROPE, 1)
        q_ref[0, h, :, 0:MLA_NOPE] = (q[:, nope] * scale).astype(BF16)
        q_ref[0, h, :, MLA_NOPE:MLA_QK_PAD] = (_rope_lanes(a, cos_t, sin_t, low_half) * scale).astype(BF16)
        k_ref[0, h, 0] = kt[nope, :].astype(BF16)
        v_ref[0, h] = v[:, h * MLA_V:(h + 1) * MLA_V].astype(BF16)


def _mla_qkv(proj, kr, pos, q_norm_w, kv_norm_w, w_uq, w_uk_t, w_uv, freq, sign, batch, tile):
    n = proj.shape[0]
    seq = n // batch
    tiles = seq // tile
    const = lambda shape: pl.BlockSpec(shape, lambda b, t: (0,) * len(shape))
    head_spec = lambda w: pl.BlockSpec((1, MLA_HEADS, tile, w), lambda b, t: (b, 0, t, 0))
    return pl.pallas_call(
        _mla_qkv_kernel,
        grid=(batch, tiles),
        in_specs=[pl.BlockSpec((tile, MLA_Q_RANK + MLA_KV_RANK), lambda b, t: (b * tiles + t, 0)),
                  pl.BlockSpec((tile, LANES), lambda b, t: (b * tiles + t, 0)),
                  pl.BlockSpec((tile, 1), lambda b, t: (b * tiles + t, 0)),
                  const((1, MLA_Q_RANK)), const((1, MLA_KV_RANK)),
                  const((MLA_Q_RANK, MLA_WIDTH + MLA_HEADS * MLA_ROPE)),
                  const((MLA_WIDTH, MLA_KV_RANK)), const((MLA_KV_RANK, MLA_WIDTH)),
                  const((1, LANES)), const((1, LANES))],
        out_specs=[head_spec(MLA_QK_PAD),
                   pl.BlockSpec((1, MLA_HEADS, 1, MLA_NOPE, tile), lambda b, t: (b, 0, t, 0, 0)),
                   pl.BlockSpec((1, 1, LANES, tile), lambda b, t: (b, t, 0, 0)), head_spec(MLA_V)],
        out_shape=[jax.ShapeDtypeStruct((batch, MLA_HEADS, seq, MLA_QK_PAD), BF16),
                   jax.ShapeDtypeStruct((batch, MLA_HEADS, tiles, MLA_NOPE, tile), BF16),
                   jax.ShapeDtypeStruct((batch, tiles, LANES, tile), BF16),
                   jax.ShapeDtypeStruct((batch, MLA_HEADS, seq, MLA_V), BF16)],
        compiler_params=_cparams(("parallel", "parallel")),
        name="mla_qkv",
    )(proj, kr, pos, q_norm_w, kv_norm_w, w_uq, w_uk_t, w_uv, freq, sign)


def _causal_attention_kernel(q_ref, k_ref, kr_ref, v_ref, o_ref, s_ref, m_ref, acc_ref, *, tq):
    chains = 2
    heads = range(q_ref.shape[1])
    tk = tq // chains
    tiles = q_ref.shape[2] // tq
    ones = jnp.ones((tk, MLA_V), BF16)

    def kv_rows(j):
        return pl.ds(pl.multiple_of(j * tk, tk), tk)

    def scores(hd, c, j, buf, tile):
        k_t = jnp.concatenate([k_ref[0, hd, j], kr_ref[0, j]], axis=0)
        s_ref[buf, hd, c] = _dot(q_ref[0, hd, kv_rows(tile * chains + c), :], k_t)

    def softmax_pv(hd, c, j, buf, diagonal):
        s = s_ref[buf, hd, c]
        if diagonal:
            row = lax.broadcasted_iota(jnp.int32, (tk, tk), 0)
            col = lax.broadcasted_iota(jnp.int32, (tk, tk), 1)
            s = jnp.where(col <= row, s, -1e30)
        m = m_ref[hd, c]
        m_new = jnp.maximum(m, jnp.max(s, axis=-1, keepdims=True))
        m_ref[hd, c] = m_new
        p = jnp.exp2(s - jnp.concatenate([m_new] * (tk // LANES), axis=1)).astype(BF16)
        alpha = jnp.exp2(m - m_new)
        pv = _dot(p, jnp.concatenate([v_ref[0, hd, kv_rows(j), :], ones], axis=1))
        for half in range(2):
            lanes = slice(half * MLA_V, (half + 1) * MLA_V)
            acc_ref[hd, c, :, lanes] = alpha * acc_ref[hd, c, :, lanes] + pv[:, lanes]

    for hd in heads:
        for c in range(chains):
            scores(hd, c, 0, 0, tile=0)

    def query_tile(qi, carry):
        m_ref[...] = jnp.full(m_ref.shape, -1e30, F32)
        acc_ref[...] = jnp.zeros(acc_ref.shape, F32)

        def pairs(count):
            def body(jj, carry):
                for u in range(count):
                    j = 2 * (count * jj + u)
                    for cur in range(2):
                        for hd in heads:
                            for c in range(chains):
                                scores(hd, c, j + cur + 1, 1 - cur, tile=qi)
                        for hd in heads:
                            for c in range(chains):
                                softmax_pv(hd, c, j + cur, cur, False)
                return carry
            return body

        lax.fori_loop(0, qi // 4, pairs(4), 0)
        lax.fori_loop(qi // 4 * 2, qi // 2, pairs(2), 0)
        lax.fori_loop(qi // 2 * 2, qi, pairs(1), 0)
        j = 2 * qi
        for hd in heads:
            scores(hd, 1, j + 1, 1, tile=qi)
        for hd in heads:
            softmax_pv(hd, 0, j, 0, True)
        for hd in heads:
            softmax_pv(hd, 1, j, 0, False)
        for hd in heads:
            for c in range(chains):
                scores(hd, c, 0, 0, tile=jnp.minimum(qi + 1, tiles - 1))
        for hd in heads:
            softmax_pv(hd, 1, j + 1, 1, True)
        for hd in heads:
            for c in range(chains):
                o_ref[kv_rows(qi * chains + c), hd * MLA_V:(hd + 1) * MLA_V] = (
                    acc_ref[hd, c, :, 0:MLA_V] / acc_ref[hd, c, :, MLA_V:]).astype(o_ref.dtype)
        return carry

    lax.fori_loop(0, tiles, query_tile, 0)


def _causal_attention(q, k, k_rope, v, tq, hp):
    batch, heads, seq, _ = q.shape
    tk = tq // 2
    per_head = lambda width: pl.BlockSpec((1, hp, seq, width), lambda b, h: (b, h, 0, 0))
    return pl.pallas_call(
        functools.partial(_causal_attention_kernel, tq=tq),
        grid=(batch, heads // hp),
        in_specs=[per_head(MLA_QK_PAD),
                  pl.BlockSpec((1, hp, seq // tk, MLA_NOPE, tk), lambda b, h: (b, h, 0, 0, 0)),
                  pl.BlockSpec((1, seq // tk, LANES, tk), lambda b, h: (b, 0, 0, 0)), per_head(MLA_V)],
        out_specs=pl.BlockSpec((seq, hp * MLA_V), lambda b, h: (b, h)),
        out_shape=jax.ShapeDtypeStruct((batch * seq, heads * MLA_V), BF16),
        scratch_shapes=[pltpu.VMEM((2, hp, 2, tk, tk), F32),
                        pltpu.VMEM((hp, 2, tk, LANES), F32),
                        pltpu.VMEM((hp, 2, tk, 2 * MLA_V), F32)],
        compiler_params=_cparams(("parallel", "parallel")),
        name="causal_attention",
    )(q, k, k_rope, v)


def _odd_tail_kernel(a_ref, p_ref, k_ref, v_ref, x_ref, w_ref, fw_ref, out_ref):
    y_c = a_ref[...].astype(F32) * _silu(p_ref[:, _O_ZC:_O_ZC + MLA_WIDTH].astype(F32))
    acc = x_ref[...] + _dot(y_c.astype(BF16), w_ref[0:MLA_WIDTH, :])
    y_m = _mem_attention(p_ref[:, _O_MQ1:_O_MQ1 + MEM_WIDTH].astype(F32),
                         p_ref[:, _O_MZ1:_O_MZ1 + MEM_WIDTH].astype(F32), k_ref[0], v_ref[0])
    y_m = jnp.concatenate(y_m, axis=1).astype(BF16)
    acc = acc + _dot(y_m, w_ref[MLA_WIDTH:ODD_OUT, :])
    out_ref[...] = _rms(acc, fw_ref[...])


def _odd_tail(attn, proj, mem_kv, x, w_out, final_w, batch, tile):
    n = x.shape[0]
    tiles = n // batch // tile
    const = lambda shape: pl.BlockSpec(shape, lambda b, t: (0,) * len(shape))
    return pl.pallas_call(
        _odd_tail_kernel,
        grid=(batch, tiles),
        in_specs=[pl.BlockSpec((tile, MLA_WIDTH), lambda b, t: (b * tiles + t, 0)),
                  pl.BlockSpec((tile, ODD_MAIN), lambda b, t: (b * tiles + t, 0)),
                  pl.BlockSpec((1, N_MEM, MEM_WIDTH), lambda b, t: (b, 0, 2)),
                  pl.BlockSpec((1, N_MEM, MEM_WIDTH), lambda b, t: (b, 0, 3)),
                  pl.BlockSpec((tile, D_MODEL), lambda b, t: (b * tiles + t, 0)),
                  _resident((ODD_OUT, D_MODEL)), const((1, D_MODEL))],
        out_specs=pl.BlockSpec((tile, D_MODEL), lambda b, t: (b * tiles + t, 0)),
        out_shape=jax.ShapeDtypeStruct((n, D_MODEL), F32),
        compiler_params=_cparams(("parallel", "parallel")),
        name="odd_tail",
    )(attn, proj, mem_kv, mem_kv, x, w_out, final_w)


def _pad_cols(w, width):
    return jnp.pad(w, ((0, 0), (0, width - w.shape[1])))


def _expansion_matrix(width_per_head):
    e = np.zeros((LANES, SSD_HEADS * width_per_head), np.float32)
    for h in range(SSD_HEADS):
        e[h, h * width_per_head:(h + 1) * width_per_head] = 1.0
    return jnp.asarray(e, BF16)


def _shift_matrix():
    s = np.zeros(((SSD_CONV - 1) * SSD_CHUNK, 2 * SSD_CHUNK), np.float32)
    for k in range(SSD_CONV - 1):
        for t in range(SSD_CHUNK):
            s[k * SSD_CHUNK + t, SSD_CHUNK + t - (SSD_CONV - 1 - k)] = 1.0
    return jnp.asarray(s, BF16)


def kernel(x, mem, positions, mem_norm_w, norm0_w, w_in0, sc_conv_w, ssd_conv_w, ssd_conv_b, ssd_dt_bias,
           ssd_a_log, ssd_d, ssd_norm_w, mem_k0, mem_v0, w_out0, norm1_w, w_in1, mla_q_norm_w,
           mla_kv_norm_w, mla_w_uq, mla_w_ukv, mem_k1, mem_v1, w_out1, final_norm_w):
    batch, seq, d = x.shape
    n = batch * seq
    x2d = x.reshape(n, d)
    row = lambda v: v.reshape(1, -1).astype(F32)

    mem_kv = _mem_kv(mem, mem_norm_w, jnp.concatenate([mem_k0, mem_v0, mem_k1, mem_v1], axis=1).astype(BF16))

    o_dt = _O_MQ0
    w_in0 = lax.optimization_barrier(w_in0.astype(BF16))
    w0_main = jnp.concatenate([w_in0[:, :o_dt], w_in0[:, o_dt + SSD_HEADS:]], axis=1)
    w0_dt = _pad_cols(w_in0[:, o_dt:o_dt + SSD_HEADS], LANES)
    proj0, dt_raw = _norm_matmul(x2d, norm0_w, w0_main, w0_dt, tm=min(1024, n), tn=2048)
    pad_heads = lambda v: _pad_cols(row(v), LANES)
    y0 = _even_mixer(proj0, dt_raw, mem_kv, sc_conv_w, ssd_conv_w, row(ssd_conv_b), pad_heads(ssd_dt_bias),
                     pad_heads(ssd_a_log), row(jnp.repeat(ssd_d, SSD_HEAD_DIM)), row(ssd_norm_w),
                     _expansion_matrix(SSD_HEAD_DIM), _expansion_matrix(SSD_CHUNK),
                     jnp.asarray(np.tril(np.ones((SSD_CHUNK, SSD_CHUNK), np.float32)), BF16),
                     _shift_matrix(),
                     batch, tile=min(256, seq))
    x1 = _residual_matmul(y0, w_out0.astype(BF16), x2d, tm=min(512, n))

    o_kr = MLA_Q_RANK + MLA_KV_RANK
    w_in1 = lax.optimization_barrier(w_in1.astype(BF16))
    w1_main = jnp.concatenate([w_in1[:, :o_kr], w_in1[:, o_kr + MLA_ROPE:]], axis=1)
    w1_kr = _pad_cols(w_in1[:, o_kr:o_kr + MLA_ROPE], LANES)
    proj1, kr = _norm_matmul(x1, norm1_w, w1_main, w1_kr, tm=min(1024, n), tn=ODD_MAIN // 2)
    wq = mla_w_uq.astype(BF16).reshape(MLA_Q_RANK, MLA_HEADS, MLA_NOPE + MLA_ROPE)
    wq = jnp.concatenate([wq[:, :, :MLA_NOPE].reshape(MLA_Q_RANK, MLA_WIDTH),
                          wq[:, :, MLA_NOPE:].reshape(MLA_Q_RANK, MLA_HEADS * MLA_ROPE)], axis=1)
    wkv = mla_w_ukv.astype(BF16).reshape(MLA_KV_RANK, MLA_HEADS, MLA_NOPE + MLA_V)
    w_uk_t = wkv[:, :, :MLA_NOPE].reshape(MLA_KV_RANK, MLA_WIDTH).T
    w_uv = wkv[:, :, MLA_NOPE:].reshape(MLA_KV_RANK, MLA_WIDTH)
    half = MLA_ROPE // 2
    inv = ROPE_THETA ** (-jnp.arange(half, dtype=F32) / half)
    freq = jnp.concatenate([inv, inv, jnp.zeros((LANES - MLA_ROPE,), F32)]).reshape(1, LANES)
    sign = np.zeros((1, LANES), np.float32)
    sign[0, :half] = -1.0
    sign[0, half:MLA_ROPE] = 1.0
    q, k, k_rope, v = _mla_qkv(proj1, kr, positions.reshape(n, 1), row(mla_q_norm_w), row(mla_kv_norm_w), wq,
                               w_uk_t, w_uv, freq, jnp.asarray(sign), batch, tile=min(512, seq))
    attn = _causal_attention(q, k, k_rope, v, tq=2 * min(512, seq), hp=1)
    out = _odd_tail(attn, proj1, mem_kv, x1, w_out1.astype(BF16), row(final_norm_w), batch, tile=min(512, seq))
    return out.reshape(batch, seq, d)
```

```python
import functools

import numpy as np
import jax
import jax.numpy as jnp
from jax import lax
from jax.experimental import pallas as pl
from jax.experimental.pallas import tpu as pltpu

F32 = jnp.float32
BF16 = jnp.bfloat16

D_MODEL = 2048
N_MEM = 256
EPS = 1e-6
SC_WIDTH = 1024
SC_KERNEL = 3
SSD_HEADS = 32
SSD_HEAD_DIM = 64
SSD_WIDTH = SSD_HEADS * SSD_HEAD_DIM
SSD_STATE = 128
SSD_GROUPS = 4
SSD_CONV = 4
SSD_CHUNK = 128
SSD_CONV_DIM = SSD_WIDTH + 2 * SSD_GROUPS * SSD_STATE
SSD_GROUP_WIDTH = SSD_WIDTH // SSD_GROUPS
MLA_HEADS = 16
MLA_Q_RANK = 768
MLA_KV_RANK = 512
MLA_NOPE = 128
MLA_ROPE = 64
MLA_V = 128
MLA_WIDTH = MLA_HEADS * MLA_V
MLA_QK_PAD = 256
ROPE_THETA = 10000.0
MEM_HEADS = 4
MEM_HEAD_DIM = 128
MEM_WIDTH = MEM_HEADS * MEM_HEAD_DIM
EVEN_MAIN = 4 * SC_WIDTH + SSD_WIDTH + SSD_CONV_DIM + 2 * MEM_WIDTH
EVEN_OUT = SC_WIDTH + SSD_WIDTH + MEM_WIDTH
ODD_MAIN = MLA_Q_RANK + MLA_KV_RANK + MLA_WIDTH + 2 * MEM_WIDTH
ODD_OUT = MLA_WIDTH + MEM_WIDTH
LANES = 128
HALO = 8

VMEM_LIMIT = 56 * 1024 * 1024


def _cparams(sem):
    return pltpu.CompilerParams(dimension_semantics=sem, vmem_limit_bytes=VMEM_LIMIT)


def _dot(a, b):
    return jnp.dot(a, b, preferred_element_type=F32)


def _dot_nt(a, b):
    return lax.dot_general(a, b, (((1,), (1,)), ((), ())), preferred_element_type=F32)


def _rms(x, w):
    return x * lax.rsqrt(jnp.mean(x * x, axis=-1, keepdims=True) + EPS) * w


def _silu(z):
    h = 0.5 * z
    return h + h * jnp.tanh(h)


def _split_bf16(v, n):
    parts = []
    r = v
    for _ in range(n):
        p = r.astype(BF16)
        parts.append(p)
        r = r - p.astype(F32)
    return parts


def _expand(pieces, e):
    out = None
    for p in pieces:
        t = _dot(p, e)
        out = t if out is None else out + t
    return out


def _mem_kv_kernel(mem_ref, nw_ref, w_ref, out_ref):
    m = _rms(mem_ref[0], nw_ref[...])
    out_ref[0] = _dot(m.astype(BF16), w_ref[...]).astype(BF16)


def _mem_kv(mem, mem_norm_w, w_kv):
    b, m, d = mem.shape
    n = w_kv.shape[1]
    tn = MEM_WIDTH
    return pl.pallas_call(
        _mem_kv_kernel,
        grid=(b, n // tn),
        in_specs=[pl.BlockSpec((1, m, d), lambda i, j: (i, 0, 0)),
                  pl.BlockSpec((1, d), lambda i, j: (0, 0)),
                  pl.BlockSpec((d, tn), lambda i, j: (0, j))],
        out_specs=pl.BlockSpec((1, m, tn), lambda i, j: (i, 0, j)),
        out_shape=jax.ShapeDtypeStruct((b, m, n), BF16),
        compiler_params=_cparams(("parallel", "arbitrary")),
        name="mem_kv",
    )(mem, mem_norm_w.reshape(1, d), w_kv)


def _norm_matmul_kernel(x_ref, nw_ref, w_ref, ws_ref, out_ref, side_ref, xn_ref):
    @pl.when(pl.program_id(1) == 0)
    def _():
        xn_ref[...] = _rms(x_ref[...], nw_ref[...]).astype(BF16)
        side_ref[...] = _dot(xn_ref[...], ws_ref[...])

    out_ref[...] = _dot(xn_ref[...], w_ref[...]).astype(out_ref.dtype)


def _norm_matmul(x, norm_w, w, w_side, tm, tn):
    n, d = x.shape
    c = w.shape[1]
    cs = w_side.shape[1]
    return pl.pallas_call(
        _norm_matmul_kernel,
        grid=(n // tm, c // tn),
        in_specs=[pl.BlockSpec((tm, d), lambda i, j: (i, 0)),
                  pl.BlockSpec((1, d), lambda i, j: (0, 0)),
                  pl.BlockSpec((d, tn), lambda i, j: (0, j)),
                  pl.BlockSpec((d, cs), lambda i, j: (0, 0))],
        out_specs=[pl.BlockSpec((tm, tn), lambda i, j: (i, j)),
                   pl.BlockSpec((tm, cs), lambda i, j: (i, 0))],
        out_shape=[jax.ShapeDtypeStruct((n, c), BF16),
                   jax.ShapeDtypeStruct((n, cs), F32)],
        scratch_shapes=[pltpu.VMEM((tm, d), BF16)],
        compiler_params=_cparams(("parallel", "arbitrary")),
        name="norm_matmul",
    )(x, norm_w.reshape(1, d), w, w_side)


def _mem_attention(mq, mz, k, v):
    outs = []
    for h in range(MEM_HEADS):
        sl = slice(h * MEM_HEAD_DIM, (h + 1) * MEM_HEAD_DIM)
        q = (mq[:, sl] * (MEM_HEAD_DIM ** -0.5)).astype(BF16)
        s = _dot_nt(q, k[:, sl])
        p = jnp.exp(s - jnp.max(s, axis=-1, keepdims=True))
        o = _dot(p.astype(BF16), v[:, sl]) * (1.0 / jnp.sum(p, axis=-1, keepdims=True))
        outs.append(o * _silu(mz[:, sl]))
    return outs


_O_SCB, _O_SCC, _O_SCV, _O_SCZ = 0, SC_WIDTH, 2 * SC_WIDTH, 3 * SC_WIDTH
_O_SSDZ = 4 * SC_WIDTH
_O_XBC = _O_SSDZ + SSD_WIDTH
_O_MQ0 = _O_XBC + SSD_CONV_DIM
_O_MZ0 = _O_MQ0 + MEM_WIDTH


def _even_mixer_kernel(p_ref, dt_ref, k_ref, v_ref, *refs):
    consts = refs[:-4]
    y_ref, cv_buf, xbc_tail, state = refs[-4:]

    @pl.when(pl.program_id(1) == 0)
    def _():
        cv_buf[:, 0:HALO, :] = jnp.zeros((cv_buf.shape[0], HALO, SC_WIDTH), F32)
        xbc_tail[...] = jnp.zeros(xbc_tail.shape, BF16)
        state[...] = jnp.zeros(state.shape, F32)

    live = [_even_mixer_tile(p_ref.at[i], dt_ref.at[i], k_ref.at[i], v_ref.at[i], *consts,
                             y_ref.at[i], cv_buf.at[i], xbc_tail.at[i], state.at[i])
            for i in range(p_ref.shape[0])]
    while live:
        live = [g for g in live if next(g, "done") != "done"]


def _even_mixer_tile(p_ref, dt_ref, k_ref, v_ref, scw_ref, xw_ref, xb_ref, dtb_ref, alog_ref, d_ref,
                     nw_ref, ehp_ref, ehk_ref, tri_ref, shift_ref, y_ref, cv_buf, xbc_tail, state):
    tile = p_ref.shape[0]

    cv_buf[HALO:HALO + tile, :] = (p_ref[:, _O_SCC:_O_SCC + SC_WIDTH].astype(F32)
                                   * p_ref[:, _O_SCV:_O_SCV + SC_WIDTH].astype(F32))
    conv = None
    for k in range(SC_KERNEL):
        off = HALO - (SC_KERNEL - 1) + k
        t = scw_ref[k:k + 1, :] * cv_buf[off:off + tile, :]
        conv = t if conv is None else conv + t
    y_a = (p_ref[:, _O_SCB:_O_SCB + SC_WIDTH].astype(F32) * conv
           * _silu(p_ref[:, _O_SCZ:_O_SCZ + SC_WIDTH].astype(F32)))
    y_ref[:, 0:SC_WIDTH] = y_a.astype(BF16)
    cv_buf[0:HALO, :] = cv_buf[tile:tile + HALO, :]
    yield

    y_m = _mem_attention(p_ref[:, _O_MQ0:_O_MQ0 + MEM_WIDTH].astype(F32),
                         p_ref[:, _O_MZ0:_O_MZ0 + MEM_WIDTH].astype(F32), k_ref[...], v_ref[...])
    for h in range(MEM_HEADS):
        c0 = SC_WIDTH + SSD_WIDTH + h * MEM_HEAD_DIM
        y_ref[:, c0:c0 + MEM_HEAD_DIM] = y_m[h].astype(BF16)
    yield

    xbc_cols = slice(_O_XBC, _O_XBC + SSD_CONV_DIM)
    a_row = -jnp.exp(alog_ref[...])
    row = lax.broadcasted_iota(jnp.int32, (SSD_CHUNK, SSD_CHUNK), 0)
    col = lax.broadcasted_iota(jnp.int32, (SSD_CHUNK, SSD_CHUNK), 1)
    causal = row >= col
    first_head = col < SSD_HEAD_DIM
    heads_per_group = SSD_HEADS // SSD_GROUPS
    for c in range(tile // SSD_CHUNK):
        r0 = c * SSD_CHUNK
        cur = p_ref[r0:r0 + SSD_CHUNK, xbc_cols]
        prev = xbc_tail[...] if c == 0 else p_ref[r0 - SSD_CHUNK:r0, xbc_cols]
        shifted = _dot(shift_ref[...], jnp.concatenate([prev, cur], axis=0))
        u = xb_ref[...] + xw_ref[SSD_CONV - 1:SSD_CONV, :] * cur.astype(F32)
        for k in range(SSD_CONV - 1):
            u = u + xw_ref[k:k + 1, :] * shifted[k * SSD_CHUNK:(k + 1) * SSD_CHUNK, :]
        xbc = _silu(u)
        yield
        xs = xbc[:, 0:SSD_WIDTH]
        bm = xbc[:, SSD_WIDTH:SSD_WIDTH + SSD_GROUPS * SSD_STATE]
        cm = xbc[:, SSD_WIDTH + SSD_GROUPS * SSD_STATE:]
        dt_in = dt_ref[r0:r0 + SSD_CHUNK, :] + dtb_ref[...]
        dt = jnp.maximum(dt_in, 0.0) + jnp.log1p(jnp.exp(-jnp.abs(dt_in)))
        dta = dt * a_row
        tri = tri_ref[...]
        a_cum = None
        for piece in _split_bf16(dta, 3):
            t = _dot(tri, piece)
            a_cum = t if a_cum is None else a_cum + t
        a_cum_t = a_cum.T
        a_last = a_cum[SSD_CHUNK - 1:SSD_CHUNK, :]
        dt_pieces = _split_bf16(dt, 2)
        exp_a_pieces = _split_bf16(jnp.exp(a_cum), 2)
        dtd_pieces = _split_bf16(dt * jnp.exp(a_last - a_cum), 2)
        a_cum_pieces = _split_bf16(a_cum, 3)
        yield
        for g in range(SSD_GROUPS):
            gs = slice(g * SSD_GROUP_WIDTH, (g + 1) * SSD_GROUP_WIDTH)
            e_hp = ehp_ref[:, gs]
            dt_e = _expand(dt_pieces, e_hp)
            exp_a_e = _expand(exp_a_pieces, e_hp)
            dtd_e = _expand(dtd_pieces, e_hp)
            hk = slice(g * heads_per_group * SSD_CHUNK, (g + 1) * heads_per_group * SSD_CHUNK)
            a_q = _expand(a_cum_pieces, ehk_ref[:, hk])
            yield
            xs_g = xs[:, gs]
            xdt = xs_g * dt_e
            bm_g = bm[:, g * SSD_STATE:(g + 1) * SSD_STATE]
            cm_g = cm[:, g * SSD_STATE:(g + 1) * SSD_STATE].astype(BF16)
            cb = _dot_nt(cm_g, bm_g.astype(BF16))
            s_enter = state[g]
            y_off = _dot(cm_g, s_enter.astype(BF16)) * exp_a_e
            state[g] = (s_enter * exp_a_e[SSD_CHUNK - 1:SSD_CHUNK, :]
                        + _dot(bm_g.T.astype(BF16), (xs_g * dtd_e).astype(BF16)))
            yield
            pairs = []
            for j in range(heads_per_group // 2):
                lmat = []
                for h in (2 * j, 2 * j + 1):
                    h_abs = g * heads_per_group + h
                    seg = a_q[:, h * SSD_CHUNK:(h + 1) * SSD_CHUNK] - a_cum_t[h_abs:h_abs + 1, :]
                    lmat.append((cb * jnp.exp(jnp.where(causal, seg, -jnp.inf))).astype(BF16))
                xp = xdt[:, 2 * j * SSD_HEAD_DIM:(2 * j + 2) * SSD_HEAD_DIM]
                rhs = jnp.concatenate([jnp.where(first_head, xp, 0.0), jnp.where(first_head, 0.0, xp)],
                                      axis=0).astype(BF16)
                pairs.append(_dot(jnp.concatenate(lmat, axis=1), rhs))
                yield
            z = p_ref[r0:r0 + SSD_CHUNK, _O_SSDZ + gs.start:_O_SSDZ + gs.stop].astype(F32)
            y_g = (jnp.concatenate(pairs, axis=1) + y_off + xs_g * d_ref[:, gs]) * _silu(z)
            y_g = y_g * lax.rsqrt(jnp.mean(y_g * y_g, axis=-1, keepdims=True) + EPS) * nw_ref[:, gs]
            c0 = SC_WIDTH + g * SSD_GROUP_WIDTH
            y_ref[r0:r0 + SSD_CHUNK, c0:c0 + SSD_GROUP_WIDTH] = y_g.astype(BF16)
            yield
    xbc_tail[...] = p_ref[tile - SSD_CHUNK:tile, xbc_cols]


def _even_mixer(proj, dt_raw, mem_kv, sc_conv_w, ssd_conv_w, ssd_conv_b, dt_bias, a_log, d_e, ssd_norm_w,
                e_hp, e_hk, tri, shift, batch, tile):
    n = proj.shape[0]
    seq = n // batch
    rows = 2 if batch % 2 == 0 else 1
    const = lambda shape: pl.BlockSpec(shape, lambda b, t: (0,) * len(shape))
    tokens = lambda width: pl.BlockSpec((rows, tile, width), lambda b, t: (b, t, 0))
    y = pl.pallas_call(
        _even_mixer_kernel,
        grid=(batch // rows, seq // tile),
        in_specs=[tokens(EVEN_MAIN), tokens(LANES),
                  pl.BlockSpec((rows, N_MEM, MEM_WIDTH), lambda b, t: (b, 0, 0)),
                  pl.BlockSpec((rows, N_MEM, MEM_WIDTH), lambda b, t: (b, 0, 1)),
                  const((SC_KERNEL, SC_WIDTH)), const((SSD_CONV, SSD_CONV_DIM)), const((1, SSD_CONV_DIM)),
                  const((1, LANES)), const((1, LANES)), const((1, SSD_WIDTH)), const((1, SSD_WIDTH)),
                  const((LANES, SSD_WIDTH)), const((LANES, SSD_HEADS * SSD_CHUNK)),
                  const((SSD_CHUNK, SSD_CHUNK)), const(((SSD_CONV - 1) * SSD_CHUNK, 2 * SSD_CHUNK))],
        out_specs=tokens(EVEN_OUT),
        out_shape=jax.ShapeDtypeStruct((batch, seq, EVEN_OUT), BF16),
        scratch_shapes=[pltpu.VMEM((rows, tile + HALO, SC_WIDTH), F32),
                        pltpu.VMEM((rows, SSD_CHUNK, SSD_CONV_DIM), BF16),
                        pltpu.VMEM((rows, SSD_GROUPS, SSD_STATE, SSD_GROUP_WIDTH), F32)],
        compiler_params=_cparams(("parallel", "arbitrary")),
        name="even_mixer",
    )(proj.reshape(batch, seq, EVEN_MAIN), dt_raw.reshape(batch, seq, LANES), mem_kv, mem_kv, sc_conv_w,
      ssd_conv_w, ssd_conv_b, dt_bias, a_log, d_e, ssd_norm_w, e_hp, e_hk, tri, shift)
    return y.reshape(n, EVEN_OUT)


def _residual_matmul_kernel(y_ref, w_ref, x_ref, out_ref):
    out_ref[...] = x_ref[...] + _dot(y_ref[...], w_ref[...])


def _resident(shape):
    return pl.BlockSpec(shape, lambda *_: (0,) * len(shape), pipeline_mode=pl.Buffered(1))


def _residual_matmul(y, w, x, tm):
    n, k = y.shape
    d = w.shape[1]
    return pl.pallas_call(
        _residual_matmul_kernel,
        grid=(n // tm,),
        in_specs=[pl.BlockSpec((tm, k), lambda i: (i, 0)),
                  _resident((k, d)),
                  pl.BlockSpec((tm, d), lambda i: (i, 0))],
        out_specs=pl.BlockSpec((tm, d), lambda i: (i, 0)),
        out_shape=jax.ShapeDtypeStruct((n, d), F32),
        compiler_params=_cparams(("parallel",)),
        name="residual_matmul",
    )(y, w, x)


_O_CQ, _O_CKV = 0, MLA_Q_RANK
_O_ZC = MLA_Q_RANK + MLA_KV_RANK
_O_MQ1 = _O_ZC + MLA_WIDTH
_O_MZ1 = _O_MQ1 + MEM_WIDTH


def _rope_lanes(a, cos_t, sin_t, low_half):
    half = MLA_ROPE // 2
    swapped = jnp.where(low_half, pltpu.roll(a, LANES - half, 1), pltpu.roll(a, half, 1))
    return a * cos_t + swapped * sin_t


def _mla_qkv_kernel(p_ref, kr_ref, pos_ref, qnw_ref, kvnw_ref, wq_ref, wkt_ref, wv_ref, freq_ref, sign_ref,
                    q_ref, k_ref, kro_ref, v_ref):
    tile = p_ref.shape[0]
    cq = _rms(p_ref[:, _O_CQ:_O_CQ + MLA_Q_RANK].astype(F32), qnw_ref[...])
    q = _dot(cq.astype(BF16), wq_ref[...])
    ckv = _rms(p_ref[:, _O_CKV:_O_CKV + MLA_KV_RANK].astype(F32), kvnw_ref[...]).astype(BF16)
    kt = _dot_nt(wkt_ref[...], ckv)
    v = _dot(ckv, wv_ref[...])
    ang = pos_ref[...].astype(F32) * freq_ref[...]
    cos_t = jnp.cos(ang)
    sin_t = jnp.sin(ang) * sign_ref[...]
    low_half = lax.broadcasted_iota(jnp.int32, (tile, LANES), 1) < MLA_ROPE // 2
    kro_ref[0, 0] = _rope_lanes(kr_ref[...], cos_t, sin_t, low_half).T.astype(BF16)
    scale = (MLA_NOPE + MLA_ROPE) ** -0.5 * np.log2(np.e)
    for h in range(MLA_HEADS):
        nope = slice(h * MLA_NOPE, (h + 1) * MLA_NOPE)
        pair = q[:, MLA_WIDTH + (h // 2) * LANES:MLA_WIDTH + (h // 2 + 1) * LANES]
        a = pair if h % 2 == 0 else pltpu.roll(pair, MLA_ROPE, 1)
        q_ref[0, h, :, 0:MLA_NOPE] = (q[:, nope] * scale).astype(BF16)
        q_ref[0, h, :, MLA_NOPE:MLA_QK_PAD] = (_rope_lanes(a, cos_t, sin_t, low_half) * scale).astype(BF16)
        k_ref[0, h, 0] = kt[nope, :].astype(BF16)
        v_ref[0, h] = v[:, h * MLA_V:(h + 1) * MLA_V].astype(BF16)


def _mla_qkv(proj, kr, pos, q_norm_w, kv_norm_w, w_uq, w_uk_t, w_uv, freq, sign, batch, tile):
    n = proj.shape[0]
    seq = n // batch
    tiles = seq // tile
    const = lambda shape: pl.BlockSpec(shape, lambda b, t: (0,) * len(shape))
    head_spec = lambda w: pl.BlockSpec((1, MLA_HEADS, tile, w), lambda b, t: (b, 0, t, 0))
    return pl.pallas_call(
        _mla_qkv_kernel,
        grid=(batch, tiles),
        in_specs=[pl.BlockSpec((tile, MLA_Q_RANK + MLA_KV_RANK), lambda b, t: (b * tiles + t, 0)),
                  pl.BlockSpec((tile, LANES), lambda b, t: (b * tiles + t, 0)),
                  pl.BlockSpec((tile, 1), lambda b, t: (b * tiles + t, 0)),
                  const((1, MLA_Q_RANK)), const((1, MLA_KV_RANK)),
                  const((MLA_Q_RANK, MLA_WIDTH + MLA_HEADS * MLA_ROPE)),
                  const((MLA_WIDTH, MLA_KV_RANK)), const((MLA_KV_RANK, MLA_WIDTH)),
                  const((1, LANES)), const((1, LANES))],
        out_specs=[head_spec(MLA_QK_PAD),
                   pl.BlockSpec((1, MLA_HEADS, 1, MLA_NOPE, tile), lambda b, t: (b, 0, t, 0, 0)),
                   pl.BlockSpec((1, 1, LANES, tile), lambda b, t: (b, t, 0, 0)), head_spec(MLA_V)],
        out_shape=[jax.ShapeDtypeStruct((batch, MLA_HEADS, seq, MLA_QK_PAD), BF16),
                   jax.ShapeDtypeStruct((batch, MLA_HEADS, tiles, MLA_NOPE, tile), BF16),
                   jax.ShapeDtypeStruct((batch, tiles, LANES, tile), BF16),
                   jax.ShapeDtypeStruct((batch, MLA_HEADS, seq, MLA_V), BF16)],
        compiler_params=_cparams(("parallel", "parallel")),
        name="mla_qkv",
    )(proj, kr, pos, q_norm_w, kv_norm_w, w_uq, w_uk_t, w_uv, freq, sign)


def _causal_attention_kernel(q_ref, k_ref, kr_ref, v_ref, o_ref, s_ref, m_ref, acc_ref, *, tq):
    chains = 2
    heads = range(q_ref.shape[1])
    tk = tq // chains
    tiles = q_ref.shape[2] // tq
    ones = jnp.ones((tk, MLA_V), BF16)

    def kv_rows(j):
        return pl.ds(pl.multiple_of(j * tk, tk), tk)

    def scores(hd, c, j, buf, tile):
        k_t = jnp.concatenate([k_ref[0, hd, j], kr_ref[0, j]], axis=0)
        s_ref[buf, hd, c] = _dot(q_ref[0, hd, kv_rows(tile * chains + c), :], k_t)

    def softmax_pv(hd, c, j, buf, diagonal):
        s = s_ref[buf, hd, c]
        if diagonal:
            row = lax.broadcasted_iota(jnp.int32, (tk, tk), 0)
            col = lax.broadcasted_iota(jnp.int32, (tk, tk), 1)
            s = jnp.where(col <= row, s, -1e30)
        m = m_ref[hd, c]
        m_new = jnp.maximum(m, jnp.max(s, axis=-1, keepdims=True))
        m_ref[hd, c] = m_new
        p = jnp.exp2(s - jnp.concatenate([m_new] * (tk // LANES), axis=1)).astype(BF16)
        alpha = jnp.exp2(m - m_new)
        pv = _dot(p, jnp.concatenate([v_ref[0, hd, kv_rows(j), :], ones], axis=1))
        for half in range(2):
            lanes = slice(half * MLA_V, (half + 1) * MLA_V)
            acc_ref[hd, c, :, lanes] = alpha * acc_ref[hd, c, :, lanes] + pv[:, lanes]

    for hd in heads:
        for c in range(chains):
            scores(hd, c, 0, 0, tile=0)

    def query_tile(qi, carry):
        m_ref[...] = jnp.full(m_ref.shape, -1e30, F32)
        acc_ref[...] = jnp.zeros(acc_ref.shape, F32)

        def pairs(count):
            def body(jj, carry):
                for u in range(count):
                    j = 2 * (count * jj + u)
                    for cur in range(2):
                        for hd in heads:
                            for c in range(chains):
                                scores(hd, c, j + cur + 1, 1 - cur, tile=qi)
                                softmax_pv(hd, c, j + cur, cur, False)
                return carry
            return body

        lax.fori_loop(0, qi // 4, pairs(4), 0)
        lax.fori_loop(qi // 4 * 2, qi // 2, pairs(2), 0)
        lax.fori_loop(qi // 2 * 2, qi, pairs(1), 0)
        j = 2 * qi
        for hd in heads:
            scores(hd, 1, j + 1, 1, tile=qi)
        nxt = jnp.minimum(qi + 1, tiles - 1)
        for hd in heads:
            softmax_pv(hd, 0, j, 0, True)
            scores(hd, 0, 0, 0, tile=nxt)
        for hd in heads:
            softmax_pv(hd, 1, j, 0, False)
            scores(hd, 1, 0, 0, tile=nxt)
        for hd in heads:
            softmax_pv(hd, 1, j + 1, 1, True)
        for hd in heads:
            for c in range(chains):
                o_ref[kv_rows(qi * chains + c), hd * MLA_V:(hd + 1) * MLA_V] = (
                    acc_ref[hd, c, :, 0:MLA_V] / acc_ref[hd, c, :, MLA_V:]).astype(o_ref.dtype)
        return carry

    lax.fori_loop(0, tiles, query_tile, 0)


def _causal_attention(q, k, k_rope, v, tq, hp):
    batch, heads, seq, _ = q.shape
    tk = tq // 2
    per_head = lambda width: pl.BlockSpec((1, hp, seq, width), lambda b, h: (b, h, 0, 0))
    return pl.pallas_call(
        functools.partial(_causal_attention_kernel, tq=tq),
        grid=(batch, heads // hp),
        in_specs=[per_head(MLA_QK_PAD),
                  pl.BlockSpec((1, hp, seq // tk, MLA_NOPE, tk), lambda b, h: (b, h, 0, 0, 0)),
                  pl.BlockSpec((1, seq // tk, LANES, tk), lambda b, h: (b, 0, 0, 0)), per_head(MLA_V)],
        out_specs=pl.BlockSpec((seq, hp * MLA_V), lambda b, h: (b, h)),
        out_shape=jax.ShapeDtypeStruct((batch * seq, heads * MLA_V), BF16),
        scratch_shapes=[pltpu.VMEM((2, hp, 2, tk, tk), F32),
                        pltpu.VMEM((hp, 2, tk, LANES), F32),
                        pltpu.VMEM((hp, 2, tk, 2 * MLA_V), F32)],
        compiler_params=_cparams(("parallel", "parallel")),
        name="causal_attention",
    )(q, k, k_rope, v)


def _odd_tail_kernel(a_ref, p_ref, k_ref, v_ref, x_ref, w_ref, fw_ref, out_ref):
    y_c = a_ref[...].astype(F32) * _silu(p_ref[:, _O_ZC:_O_ZC + MLA_WIDTH].astype(F32))
    acc = x_ref[...] + _dot(y_c.astype(BF16), w_ref[0:MLA_WIDTH, :])
    y_m = _mem_attention(p_ref[:, _O_MQ1:_O_MQ1 + MEM_WIDTH].astype(F32),
                         p_ref[:, _O_MZ1:_O_MZ1 + MEM_WIDTH].astype(F32), k_ref[0], v_ref[0])
    y_m = jnp.concatenate(y_m, axis=1).astype(BF16)
    acc = acc + _dot(y_m, w_ref[MLA_WIDTH:ODD_OUT, :])
    out_ref[...] = _rms(acc, fw_ref[...])


def _odd_tail(attn, proj, mem_kv, x, w_out, final_w, batch, tile):
    n = x.shape[0]
    tiles = n // batch // tile
    const = lambda shape: pl.BlockSpec(shape, lambda b, t: (0,) * len(shape))
    return pl.pallas_call(
        _odd_tail_kernel,
        grid=(batch, tiles),
        in_specs=[pl.BlockSpec((tile, MLA_WIDTH), lambda b, t: (b * tiles + t, 0)),
                  pl.BlockSpec((tile, ODD_MAIN), lambda b, t: (b * tiles + t, 0)),
                  pl.BlockSpec((1, N_MEM, MEM_WIDTH), lambda b, t: (b, 0, 2)),
                  pl.BlockSpec((1, N_MEM, MEM_WIDTH), lambda b, t: (b, 0, 3)),
                  pl.BlockSpec((tile, D_MODEL), lambda b, t: (b * tiles + t, 0)),
                  _resident((ODD_OUT, D_MODEL)), const((1, D_MODEL))],
        out_specs=pl.BlockSpec((tile, D_MODEL), lambda b, t: (b * tiles + t, 0)),
        out_shape=jax.ShapeDtypeStruct((n, D_MODEL), F32),
        compiler_params=_cparams(("parallel", "parallel")),
        name="odd_tail",
    )(attn, proj, mem_kv, mem_kv, x, w_out, final_w)


def _pad_cols(w, width):
    return jnp.pad(w, ((0, 0), (0, width - w.shape[1])))


def _expansion_matrix(width_per_head):
    e = np.zeros((LANES, SSD_HEADS * width_per_head), np.float32)
    for h in range(SSD_HEADS):
        e[h, h * width_per_head:(h + 1) * width_per_head] = 1.0
    return jnp.asarray(e, BF16)


def _shift_matrix():
    s = np.zeros(((SSD_CONV - 1) * SSD_CHUNK, 2 * SSD_CHUNK), np.float32)
    for k in range(SSD_CONV - 1):
        for t in range(SSD_CHUNK):
            s[k * SSD_CHUNK + t, SSD_CHUNK + t - (SSD_CONV - 1 - k)] = 1.0
    return jnp.asarray(s, BF16)


def kernel(x, mem, positions, mem_norm_w, norm0_w, w_in0, sc_conv_w, ssd_conv_w, ssd_conv_b, ssd_dt_bias,
           ssd_a_log, ssd_d, ssd_norm_w, mem_k0, mem_v0, w_out0, norm1_w, w_in1, mla_q_norm_w,
           mla_kv_norm_w, mla_w_uq, mla_w_ukv, mem_k1, mem_v1, w_out1, final_norm_w):
    batch, seq, d = x.shape
    n = batch * seq
    x2d = x.reshape(n, d)
    row = lambda v: v.reshape(1, -1).astype(F32)

    mem_kv = _mem_kv(mem, mem_norm_w, jnp.concatenate([mem_k0, mem_v0, mem_k1, mem_v1], axis=1).astype(BF16))

    o_dt = _O_MQ0
    w_in0 = w_in0.astype(BF16)
    w0_main = jnp.concatenate([w_in0[:, :o_dt], w_in0[:, o_dt + SSD_HEADS:]], axis=1)
    w0_dt = _pad_cols(w_in0[:, o_dt:o_dt + SSD_HEADS], LANES)
    proj0, dt_raw = _norm_matmul(x2d, norm0_w, w0_main, w0_dt, tm=min(1024, n), tn=2048)
    pad_heads = lambda v: _pad_cols(row(v), LANES)
    y0 = _even_mixer(proj0, dt_raw, mem_kv, sc_conv_w, ssd_conv_w, row(ssd_conv_b), pad_heads(ssd_dt_bias),
                     pad_heads(ssd_a_log), row(jnp.repeat(ssd_d, SSD_HEAD_DIM)), row(ssd_norm_w),
                     _expansion_matrix(SSD_HEAD_DIM), _expansion_matrix(SSD_CHUNK),
                     jnp.asarray(np.tril(np.ones((SSD_CHUNK, SSD_CHUNK), np.float32)), BF16),
                     _shift_matrix(),
                     batch, tile=min(256, seq))
    x1 = _residual_matmul(y0, w_out0.astype(BF16), x2d, tm=min(512, n))

    o_kr = MLA_Q_RANK + MLA_KV_RANK
    w_in1 = w_in1.astype(BF16)
    w1_main = jnp.concatenate([w_in1[:, :o_kr], w_in1[:, o_kr + MLA_ROPE:]], axis=1)
    w1_kr = _pad_cols(w_in1[:, o_kr:o_kr + MLA_ROPE], LANES)
    proj1, kr = _norm_matmul(x1, norm1_w, w1_main, w1_kr, tm=min(1024, n), tn=ODD_MAIN // 2)
    wq = mla_w_uq.astype(BF16).reshape(MLA_Q_RANK, MLA_HEADS, MLA_NOPE + MLA_ROPE)
    wq = jnp.concatenate([wq[:, :, :MLA_NOPE].reshape(MLA_Q_RANK, MLA_WIDTH),
                          wq[:, :, MLA_NOPE:].reshape(MLA_Q_RANK, MLA_HEADS * MLA_ROPE)], axis=1)
    wkv = mla_w_ukv.astype(BF16).reshape(MLA_KV_RANK, MLA_HEADS, MLA_NOPE + MLA_V)
    w_uk_t = wkv[:, :, :MLA_NOPE].reshape(MLA_KV_RANK, MLA_WIDTH).T
    w_uv = wkv[:, :, MLA_NOPE:].reshape(MLA_KV_RANK, MLA_WIDTH)
    half = MLA_ROPE // 2
    inv = ROPE_THETA ** (-jnp.arange(half, dtype=F32) / half)
    freq = jnp.concatenate([inv, inv, jnp.zeros((LANES - MLA_ROPE,), F32)]).reshape(1, LANES)
    sign = np.zeros((1, LANES), np.float32)
    sign[0, :half] = -1.0
    sign[0, half:MLA_ROPE] = 1.0
    q, k, k_rope, v = _mla_qkv(proj1, kr, positions.reshape(n, 1), row(mla_q_norm_w), row(mla_kv_norm_w), wq,
                               w_uk_t, w_uv, freq, jnp.asarray(sign), batch, tile=min(512, seq))
    attn = _causal_attention(q, k, k_rope, v, tq=2 * min(512, seq), hp=1)
    out = _odd_tail(attn, proj1, mem_kv, x1, w_out1.astype(BF16), row(final_norm_w), batch, tile=min(512, seq))
    return out.reshape(batch, seq, d)
```
